```python
import math
import jax
import jax.numpy as jnp
from jax import lax
import numpy as np

D_MODEL = 1024
BATCH = 4
SEQ = 4096
DEPTH = 4

CHUNK = 64
N_META = 16
N_MIXERS = 4
NORM_EPS = 1e-6

S5_GROUP = 16
S5_GROUPS = D_MODEL // S5_GROUP
S5_STATE = 64
S5_DT_MIN = 1e-3
S5_DT_MAX = 1e-1

RG_WIDTH = (4 * D_MODEL // 3) // 64 * 64
RG_BLOCKS = 16
RG_BLOCK = RG_WIDTH // RG_BLOCKS
RG_CONV = 4
RG_C = 8.0

SB_HEADS = 16
SB_HEAD_DIM = D_MODEL // SB_HEADS
SB_QBLOCK = 128

HG_HEADS = 8
HG_HEAD_DIM = D_MODEL // HG_HEADS
HG_BLOCK = math.gcd(CHUNK, N_META)

D_FF = (8 * D_MODEL // 3 + 127) // 128 * 128
FFN_CONV = 3

kernel_name = 'hybrid_streaming_encoder'


def n_uses(m):
    return len(range(m, DEPTH, N_MIXERS))


def rms_norm(x, g):
    xf = x.astype(jnp.float32)
    y = xf * lax.rsqrt(jnp.mean(xf * xf, axis=-1, keepdims=True) + NORM_EPS)
    return (y * g.astype(jnp.float32)).astype(x.dtype)


def causal_dwconv(x, w, b):
    width, ch = w.shape
    y = lax.conv_general_dilated(x, w[:, None, :].astype(x.dtype), window_strides=(1,),
                                 padding=[(width - 1, 0)], dimension_numbers=('NWC', 'WIO', 'NWC'),
                                 feature_group_count=ch)
    return y + b.astype(x.dtype)


def _lin_combine(e1, e2):
    a1, b1 = e1
    a2, b2 = e2
    return a1 * a2, a2 * b1 + b2


def _complex_lin_combine(e1, e2):
    a1r, a1i, b1r, b1i = e1
    a2r, a2i, b2r, b2i = e2
    return (a2r * a1r - a2i * a1i, a2r * a1i + a2i * a1r,
            a2r * b1r - a2i * b1i + b2r, a2r * b1i + a2i * b1r + b2i)


def s5_mixer(xn, w_in, lam_re, lam_im, log_dt, b_re, b_im, c_re, c_im, d_skip, w_glu):
    f32 = jnp.float32
    bsz, L, _ = xn.shape
    u = (xn @ w_in).reshape(bsz, L, S5_GROUPS, S5_GROUP).astype(f32)
    lr = lam_re.astype(f32)
    li = lam_im.astype(f32)
    dt = jnp.exp(log_dt.astype(f32))[:, None]
    mag = jnp.exp(lr * dt)
    ang = li * dt
    ab_re = mag * jnp.cos(ang)
    ab_im = mag * jnp.sin(ang)
    den = lr * lr + li * li
    cr = ((ab_re - 1.0) * lr + ab_im * li) / den
    ci = (ab_im * lr - (ab_re - 1.0) * li) / den
    br = b_re.astype(f32)
    bi = b_im.astype(f32)
    bb_re = cr[..., None] * br - ci[..., None] * bi
    bb_im = cr[..., None] * bi + ci[..., None] * br
    bu_re = jnp.einsum('gph,blgh->blgp', bb_re, u)
    bu_im = jnp.einsum('gph,blgh->blgp', bb_im, u)
    a_re = jnp.broadcast_to(ab_re[None, None], (1, L, S5_GROUPS, S5_STATE))
    a_im = jnp.broadcast_to(ab_im[None, None], (1, L, S5_GROUPS, S5_STATE))
    _, _, s_re, s_im = lax.associative_scan(_complex_lin_combine, (a_re, a_im, bu_re, bu_im), axis=1)
    y = (jnp.einsum('ghp,blgp->blgh', c_re.astype(f32), s_re)
         - jnp.einsum('ghp,blgp->blgh', c_im.astype(f32), s_im)
         + d_skip.astype(f32).reshape(S5_GROUPS, S5_GROUP) * u)
    y = jax.nn.gelu(y.reshape(bsz, L, D_MODEL).astype(xn.dtype))
    a, g = jnp.split(y @ w_glu, 2, axis=-1)
    return a * jax.nn.sigmoid(g)


def rglru_mixer(xn, w_in, conv_w, conv_b, w_a, b_a, w_i, b_i, lam, w_out):
    f32 = jnp.float32
    bsz, L, _ = xn.shape
    gate, xb = jnp.split(xn @ w_in, 2, axis=-1)
    xb = causal_dwconv(xb, conv_w, conv_b)
    xblk = xb.reshape(bsz, L, RG_BLOCKS, RG_BLOCK)
    r = jax.nn.sigmoid(jnp.einsum('blnj,njk->blnk', xblk, w_a).reshape(bsz, L, RG_WIDTH) + b_a)
    i = jax.nn.sigmoid(jnp.einsum('blnj,njk->blnk', xblk, w_i).reshape(bsz, L, RG_WIDTH) + b_i)
    log_a = (-RG_C * r.astype(f32)) * jax.nn.softplus(-lam.astype(f32))
    a = jnp.exp(log_a)
    b = jnp.sqrt(-jnp.expm1(2.0 * log_a)) * (i * xb).astype(f32)
    _, h = lax.associative_scan(_lin_combine, (a, b), axis=1)
    y = h.astype(xn.dtype) * jax.nn.gelu(gate)
    return y @ w_out


def stick_breaking_mixer(xn, w_qkv, q_norm, k_norm, w_out):
    f32 = jnp.float32
    bsz, L, _ = xn.shape
    q, k, v = jnp.split(xn @ w_qkv, 3, axis=-1)

    def heads(t):
        return t.reshape(bsz, L, SB_HEADS, SB_HEAD_DIM).transpose(0, 2, 1, 3)

    q = rms_norm(heads(q), q_norm)
    k = rms_norm(heads(k), k_norm)
    v = heads(v)
    n_blk = -(-L // SB_QBLOCK)
    lp = n_blk * SB_QBLOCK
    pad = ((0, 0), (0, 0), (0, lp - L), (0, 0))
    q, k, v = jnp.pad(q, pad), jnp.pad(k, pad), jnp.pad(v, pad)
    scale = SB_HEAD_DIM ** -0.5
    outs = []
    for blk in range(n_blk):
        q0 = blk * SB_QBLOCK
        q1 = q0 + SB_QBLOCK
        z = jnp.einsum('bhqd,bhkd->bhqk', q[:, :, q0:q1], k[:, :, :q1]).astype(f32) * scale
        past = jnp.arange(q1)[None, :] < jnp.arange(q0, q1)[:, None]
        log_keep = jnp.where(past, jax.nn.log_sigmoid(-z), 0.0)
        between = lax.cumsum(log_keep, axis=3, reverse=True) - log_keep
        w = jnp.where(past, jnp.exp(jax.nn.log_sigmoid(z) + between), 0.0)
        outs.append(jnp.einsum('bhqk,bhkd->bhqd', w.astype(v.dtype), v[:, :, :q1]))
    o = jnp.concatenate(outs, axis=2)[:, :, :L]
    return o.transpose(0, 2, 1, 3).reshape(bsz, L, D_MODEL) @ w_out


def hgrn2_mixer(xn, w_in, gamma, layer, o_norm, w_out):
    f32 = jnp.float32
    bsz, L, _ = xn.shape
    q, fz, i, g = jnp.split(xn @ w_in, 4, axis=-1)
    lb = jnp.sum(jax.nn.softmax(gamma.astype(f32), axis=0)[:layer], axis=0)
    log_f = jnp.logaddexp(jnp.log(lb), jnp.log1p(-lb) + jax.nn.log_sigmoid(fz.astype(f32)))
    k = -jnp.expm1(log_f)
    nb = L // HG_BLOCK

    def blocks(t):
        t = t.astype(f32).reshape(bsz, nb, HG_BLOCK, HG_HEADS, HG_HEAD_DIM)
        return t.transpose(1, 0, 3, 2, 4)

    qc, kc, vc, gc = blocks(q), blocks(k), blocks(i), blocks(log_f)
    cum = jnp.cumsum(gc, axis=3)
    last = cum[:, :, :, -1:, :]
    q_dec = qc * jnp.exp(cum)
    k_inv = kc * jnp.exp(-cum)
    k_end = kc * jnp.exp(last - cum)
    causal = jnp.tril(jnp.ones((HG_BLOCK, HG_BLOCK), dtype=bool))
    scores = jnp.where(causal, jnp.einsum('nbhtd,nbhsd->nbhts', q_dec, k_inv), 0.0)
    o_intra = jnp.einsum('nbhts,nbhse->nbhte', scores, vc)

    def step(state, inp):
        qd, ke, vv, dec = inp
        o = jnp.einsum('bhtd,bhde->bhte', qd, state)
        state = state * dec[..., None] + jnp.einsum('bhtd,bhte->bhde', ke, vv)
        return state, o

    s0 = jnp.zeros((bsz, HG_HEADS, HG_HEAD_DIM, HG_HEAD_DIM), f32)
    _, o_inter = lax.scan(step, s0, (q_dec, k_end, vc, jnp.exp(last[:, :, :, 0, :])))
    o = (o_intra + o_inter).transpose(1, 0, 3, 2, 4).reshape(bsz, L, HG_HEADS, HG_HEAD_DIM)
    o = rms_norm(o.astype(xn.dtype), o_norm.reshape(HG_HEADS, HG_HEAD_DIM)).reshape(bsz, L, D_MODEL)
    return (o * jax.nn.silu(g)) @ w_out


def conv_ffn(xn, w_up, conv_w, conv_b, w_down):
    h = causal_dwconv(xn @ w_up, conv_w, conv_b)
    a, b = jnp.split(h, 2, axis=-1)
    return (jax.nn.silu(a) * b) @ w_down


def setup_inputs(seed: int = 0):
    key = jax.random.key(seed)
    ks = list(jax.random.split(key, 40))
    f32 = jnp.float32

    def nrm(shape, scale):
        return jax.random.normal(ks.pop(), shape, f32) * scale

    def gain(shape):
        return 1.0 + nrm(shape, 0.01)

    na, nb, nc, nd = [n_uses(m) for m in range(N_MIXERS)]
    d = D_MODEL
    x = nrm((BATCH, SEQ, d), 1.0)
    meta_tokens = nrm((N_META, d), 1.0)
    norm_mix = gain((DEPTH, d))
    norm_ffn = gain((DEPTH, d))
    s5_w_in = nrm((na, d, d), d ** -0.5)
    s5_lam_re = -0.5 + nrm((na, S5_GROUPS, S5_STATE), 0.01)
    s5_lam_im = (jnp.broadcast_to(math.pi * jnp.arange(S5_STATE, dtype=f32), (na, S5_GROUPS, S5_STATE))
                 + nrm((na, S5_GROUPS, S5_STATE), 0.01))
    s5_log_dt = math.log(S5_DT_MIN) + jax.random.uniform(ks.pop(), (na, S5_GROUPS), f32) * (
        math.log(S5_DT_MAX) - math.log(S5_DT_MIN))
    s5_b_re = nrm((na, S5_GROUPS, S5_STATE, S5_GROUP), (2 * S5_GROUP) ** -0.5)
    s5_b_im = nrm((na, S5_GROUPS, S5_STATE, S5_GROUP), (2 * S5_GROUP) ** -0.5)
    s5_c_re = nrm((na, S5_GROUPS, S5_GROUP, S5_STATE), S5_STATE ** -0.5)
    s5_c_im = nrm((na, S5_GROUPS, S5_GROUP, S5_STATE), S5_STATE ** -0.5)
    s5_d = nrm((na, d), 1.0)
    s5_w_glu = nrm((na, d, 2 * d), d ** -0.5)
    rg_w_in = nrm((nb, d, 2 * RG_WIDTH), d ** -0.5)
    rg_conv_w = nrm((nb, RG_CONV, RG_WIDTH), RG_CONV ** -0.5)
    rg_conv_b = nrm((nb, RG_WIDTH), 0.01)
    rg_w_a = nrm((nb, RG_BLOCKS, RG_BLOCK, RG_BLOCK), RG_BLOCK ** -0.5)
    rg_b_a = nrm((nb, RG_WIDTH), 0.01)
    rg_w_i = nrm((nb, RG_BLOCKS, RG_BLOCK, RG_BLOCK), RG_BLOCK ** -0.5)
    rg_b_i = nrm((nb, RG_WIDTH), 0.01)
    s = jax.random.uniform(ks.pop(), (nb, RG_WIDTH), f32, minval=0.9, maxval=0.999) ** (1.0 / RG_C)
    rg_lambda = jnp.log(s) - jnp.log1p(-s)
    rg_w_out = nrm((nb, RG_WIDTH, d), RG_WIDTH ** -0.5)
    sb_w_qkv = nrm((nc, d, 3 * d), d ** -0.5)
    sb_q_norm = gain((nc, SB_HEAD_DIM))
    sb_k_norm = gain((nc, SB_HEAD_DIM))
    sb_w_out = nrm((nc, d, d), d ** -0.5)
    hg_w_in = nrm((nd, d, 4 * d), d ** -0.5)
    hg_gamma = nrm((DEPTH, d), 0.1)
    hg_o_norm = gain((nd, d))
    hg_w_out = nrm((nd, d, d), d ** -0.5)
    ffn_w_up = nrm((DEPTH, d, 2 * D_FF), d ** -0.5)
    ffn_conv_w = nrm((DEPTH, FFN_CONV, 2 * D_FF), FFN_CONV ** -0.5)
    ffn_conv_b = nrm((DEPTH, 2 * D_FF), 0.01)
    ffn_w_down = nrm((DEPTH, D_FF, d), D_FF ** -0.5)
    return {'x': x, 'meta_tokens': meta_tokens, 'norm_mix': norm_mix, 'norm_ffn': norm_ffn,
            's5_w_in': s5_w_in, 's5_lam_re': s5_lam_re, 's5_lam_im': s5_lam_im, 's5_log_dt': s5_log_dt,
            's5_b_re': s5_b_re, 's5_b_im': s5_b_im, 's5_c_re': s5_c_re, 's5_c_im': s5_c_im,
            's5_d': s5_d, 's5_w_glu': s5_w_glu,
            'rg_w_in': rg_w_in, 'rg_conv_w': rg_conv_w, 'rg_conv_b': rg_conv_b, 'rg_w_a': rg_w_a,
            'rg_b_a': rg_b_a, 'rg_w_i': rg_w_i, 'rg_b_i': rg_b_i, 'rg_lambda': rg_lambda, 'rg_w_out': rg_w_out,
            'sb_w_qkv': sb_w_qkv, 'sb_q_norm': sb_q_norm, 'sb_k_norm': sb_k_norm, 'sb_w_out': sb_w_out,
            'hg_w_in': hg_w_in, 'hg_gamma': hg_gamma, 'hg_o_norm': hg_o_norm, 'hg_w_out': hg_w_out,
            'ffn_w_up': ffn_w_up, 'ffn_conv_w': ffn_conv_w, 'ffn_conv_b': ffn_conv_b, 'ffn_w_down': ffn_w_down}


def reference(x, meta_tokens, norm_mix, norm_ffn,
              s5_w_in, s5_lam_re, s5_lam_im, s5_log_dt, s5_b_re, s5_b_im, s5_c_re, s5_c_im, s5_d, s5_w_glu,
              rg_w_in, rg_conv_w, rg_conv_b, rg_w_a, rg_b_a, rg_w_i, rg_b_i, rg_lambda, rg_w_out,
              sb_w_qkv, sb_q_norm, sb_k_norm, sb_w_out,
              hg_w_in, hg_gamma, hg_o_norm, hg_w_out,
              ffn_w_up, ffn_conv_w, ffn_conv_b, ffn_w_down):
    bsz = x.shape[0]
    meta = jnp.broadcast_to(meta_tokens.astype(x.dtype)[None], (bsz, N_META, D_MODEL))
    h = jnp.concatenate([meta, x], axis=1)
    for layer in range(DEPTH):
        m, j = layer % N_MIXERS, layer // N_MIXERS
        hn = rms_norm(h, norm_mix[layer])
        if m == 0:
            y = s5_mixer(hn, s5_w_in[j], s5_lam_re[j], s5_lam_im[j], s5_log_dt[j], s5_b_re[j], s5_b_im[j],
                         s5_c_re[j], s5_c_im[j], s5_d[j], s5_w_glu[j])
        elif m == 1:
            y = rglru_mixer(hn, rg_w_in[j], rg_conv_w[j], rg_conv_b[j], rg_w_a[j], rg_b_a[j], rg_w_i[j],
                            rg_b_i[j], rg_lambda[j], rg_w_out[j])
        elif m == 2:
            y = stick_breaking_mixer(hn, sb_w_qkv[j], sb_q_norm[j], sb_k_norm[j], sb_w_out[j])
        else:
            y = hgrn2_mixer(hn, hg_w_in[j], hg_gamma, layer, hg_o_norm[j], hg_w_out[j])
        h = h + y.astype(h.dtype)
        hn = rms_norm(h, norm_ffn[layer])
        h = h + conv_ffn(hn, ffn_w_up[layer], ffn_conv_w[layer], ffn_conv_b[layer],
                         ffn_w_down[layer]).astype(h.dtype)
    return h[:, N_META:]
```

```python
import functools
import math

import jax
import jax.numpy as jnp
from jax import lax
from jax.experimental import pallas as pl
from jax.experimental.pallas import tpu as pltpu

F32 = jnp.float32
BF16 = jnp.bfloat16

D_MODEL = 1024
N_META = 16
NORM_EPS = 1e-6
LANES = 128
SUBLANES = 8
SEQ_TILE = 256
ROW_TILE = 512
VMEM_LIMIT_BYTES = 56 * 1024 * 1024

S5_GROUP = 16
S5_STATE = 64
RG_BLOCKS = 16
RG_C = 8.0
SB_HEAD_DIM = 64
HG_HEAD_DIM = 128
HG_HEADS = D_MODEL // HG_HEAD_DIM
HG_BLOCK = 16


def _cparams(*sem):
    return pltpu.CompilerParams(dimension_semantics=sem, vmem_limit_bytes=VMEM_LIMIT_BYTES)


def _const_spec(shape):
    nd = len(shape)
    return pl.BlockSpec(shape, lambda *_: (0,) * nd)


def _rms(x, g):
    return x * lax.rsqrt(jnp.mean(x * x, axis=-1, keepdims=True) + NORM_EPS) * g


def _dot(a, b):
    return jnp.dot(a, b, preferred_element_type=F32)


def _dot_nt(a, b):
    return lax.dot_general(a, b, (((1,), (1,)), ((), ())), preferred_element_type=F32)


def _dot_tn(a, b):
    return lax.dot_general(a, b, (((0,), (0,)), ((), ())), preferred_element_type=F32)


def _shift_rows(x, k, row, fill=0.0):
    n = x.shape[0]
    if k % SUBLANES == 0:
        return jnp.concatenate([jnp.full((k,) + x.shape[1:], fill, x.dtype), x[: n - k]], axis=0)
    return jnp.where(row >= k, pltpu.roll(x, k, 0), fill)


def _chunks(n, width):
    out, c = [], 0
    while c < n:
        w = min(width, n - c)
        out.append((c, w))
        c += w
    return out


def _proj_kernel(x_ref, g_ref, w_ref, o_ref):
    xn = _rms(x_ref[...], g_ref[...]).astype(BF16)
    for c, w in _chunks(o_ref.shape[-1], 512):
        o_ref[:, c:c + w] = _dot(xn, w_ref[:, c:c + w]).astype(o_ref.dtype)


def _norm_proj(h2, gain, w, out_dtype):
    m, d = h2.shape
    n = w.shape[1]
    tm = ROW_TILE if m % ROW_TILE == 0 else SEQ_TILE
    return pl.pallas_call(
        _proj_kernel,
        grid=(m // tm,),
        in_specs=[pl.BlockSpec((tm, d), lambda i: (i, 0)), _const_spec((1, d)), _const_spec((d, n))],
        out_specs=pl.BlockSpec((tm, n), lambda i: (i, 0)),
        out_shape=jax.ShapeDtypeStruct((m, n), out_dtype),
        compiler_params=_cparams("parallel"),
        name="norm_proj",
    )(h2, gain.reshape(1, d), w)


def _qkv_kernel(x_ref, g_ref, w_ref, avg_ref, qg_ref, kg_ref, o_ref):
    xn = _rms(x_ref[...], g_ref[...]).astype(BF16)
    d = x_ref.shape[-1]
    for c, w in _chunks(o_ref.shape[-1], 512):
        r = _dot(xn, w_ref[:, c:c + w])
        if c < 2 * d:
            ms = _dot((r * r).astype(BF16), avg_ref[...])
            gain = qg_ref[...] if c < d else kg_ref[...]
            r = r * lax.rsqrt(ms + NORM_EPS) * gain
        o_ref[:, c:c + w] = r.astype(o_ref.dtype)


def _qkv_proj(h2, gain, w, q_norm, k_norm):
    m, d = h2.shape
    n = w.shape[1]
    tm = ROW_TILE if m % ROW_TILE == 0 else SEQ_TILE
    heads_per_chunk = 512 // SB_HEAD_DIM
    blk = jnp.arange(512) // SB_HEAD_DIM
    avg = jnp.where(blk[:, None] == blk[None, :], 1.0 / SB_HEAD_DIM, 0.0).astype(BF16)
    scale = SB_HEAD_DIM ** -0.5
    qg = jnp.tile(q_norm.astype(F32) * scale, heads_per_chunk).reshape(1, 512)
    kg = jnp.tile(k_norm.astype(F32), heads_per_chunk).reshape(1, 512)
    return pl.pallas_call(
        _qkv_kernel,
        grid=(m // tm,),
        in_specs=[pl.BlockSpec((tm, d), lambda i: (i, 0)), _const_spec((1, d)), _const_spec((d, n)),
                  _const_spec((512, 512)), _const_spec((1, 512)), _const_spec((1, 512))],
        out_specs=pl.BlockSpec((tm, n), lambda i: (i, 0)),
        out_shape=jax.ShapeDtypeStruct((m, n), BF16),
        compiler_params=_cparams("parallel"),
        name="qkv_proj",
    )(h2, gain.reshape(1, d), w, avg, qg, kg)


def _out_kernel(h_ref, y_ref, w_ref, o_ref):
    y = y_ref[...]
    for c, w in _chunks(o_ref.shape[-1], 512):
        o_ref[:, c:c + w] = h_ref[:, c:c + w] + _dot(y, w_ref[:, c:c + w])


def _out_proj(h2, y2, w):
    m, d = h2.shape
    k = y2.shape[1]
    tm = ROW_TILE if m % ROW_TILE == 0 else SEQ_TILE
    return pl.pallas_call(
        _out_kernel,
        grid=(m // tm,),
        in_specs=[pl.BlockSpec((tm, d), lambda i: (i, 0)), pl.BlockSpec((tm, k), lambda i: (i, 0)),
                  _const_spec((k, d))],
        out_specs=pl.BlockSpec((tm, d), lambda i: (i, 0)),
        out_shape=jax.ShapeDtypeStruct((m, d), F32),
        compiler_params=_cparams("parallel"),
        name="out_proj",
    )(h2, y2, w)


def _glu_out_kernel(h_ref, y_ref, w_ref, o_ref):
    y = y_ref[...]
    d = o_ref.shape[-1]
    for c, w in _chunks(d, 512):
        a = _dot(y, w_ref[:, c:c + w])
        g = _dot(y, w_ref[:, d + c:d + c + w])
        o_ref[:, c:c + w] = h_ref[:, c:c + w] + a * jax.nn.sigmoid(g)


def _glu_out_proj(h2, y2, w):
    m, d = h2.shape
    k = y2.shape[1]
    tm = ROW_TILE if m % ROW_TILE == 0 else SEQ_TILE
    return pl.pallas_call(
        _glu_out_kernel,
        grid=(m // tm,),
        in_specs=[pl.BlockSpec((tm, d), lambda i: (i, 0)), pl.BlockSpec((tm, k), lambda i: (i, 0)),
                  _const_spec((k, 2 * d))],
        out_specs=pl.BlockSpec((tm, d), lambda i: (i, 0)),
        out_shape=jax.ShapeDtypeStruct((m, d), F32),
        compiler_params=_cparams("parallel"),
        name="glu_out_proj",
    )(h2, y2, w)


def _ffn_kernel(h_ref, hp_ref, g_ref, wu_ref, cw_ref, cb_ref, wd_ref, o_ref, acc_ref, *, d_ff, halo):
    i = pl.program_id(1)
    xc = h_ref[0]
    tm = xc.shape[0]
    x = jnp.concatenate([hp_ref[0], xc], axis=0)
    row = lax.broadcasted_iota(jnp.int32, (halo + tm, 1), 0)
    valid = jnp.logical_or(row >= halo, i > 0)
    xn = jnp.where(valid, _rms(x, g_ref[...]), 0.0).astype(BF16)

    def conv(u, c, w):
        u1 = pltpu.roll(u, 1, 0)
        u2 = pltpu.roll(u, 2, 0)
        return (cb_ref[:, c:c + w] + cw_ref[2:3, c:c + w] * u[halo:] + cw_ref[1:2, c:c + w] * u1[halo:]
                + cw_ref[0:1, c:c + w] * u2[halo:])

    for n, (c, w) in enumerate(_chunks(d_ff, 256)):
        a = conv(_dot(xn, wu_ref[:, c:c + w]), c, w)
        b = conv(_dot(xn, wu_ref[:, d_ff + c:d_ff + c + w]), d_ff + c, w)
        gated = (a * jax.nn.sigmoid(a) * b).astype(BF16)
        part = _dot(gated, wd_ref[c:c + w, :])
        if n == 0:
            acc_ref[...] = xc + part
        else:
            acc_ref[...] += part
    o_ref[0] = acc_ref[...]


def _conv_ffn(h, gain, w_up, conv_w, conv_b, w_down):
    b, lp, d = h.shape
    d_ff = w_down.shape[0]
    tm = SEQ_TILE
    halo = SUBLANES
    hb = tm // halo
    return pl.pallas_call(
        functools.partial(_ffn_kernel, d_ff=d_ff, halo=halo),
        grid=(b, lp // tm),
        in_specs=[pl.BlockSpec((1, tm, d), lambda bi, i: (bi, i, 0)),
                  pl.BlockSpec((1, halo, d), lambda bi, i: (bi, jnp.maximum(i * hb - 1, 0), 0)),
                  _const_spec((1, d)), _const_spec((d, 2 * d_ff)), _const_spec((3, 2 * d_ff)),
                  _const_spec((1, 2 * d_ff)), _const_spec((d_ff, d))],
        out_specs=pl.BlockSpec((1, tm, d), lambda bi, i: (bi, i, 0)),
        out_shape=jax.ShapeDtypeStruct((b, lp, d), F32),
        scratch_shapes=[pltpu.VMEM((tm, d), F32)],
        compiler_params=_cparams("parallel", "parallel"),
        name="conv_ffn",
    )(h, h, gain.reshape(1, d), w_up, conv_w, conv_b.reshape(1, -1), w_down)


def _s5_kernel(u_ref, b_ref, c_ref, a_ref, d_ref, o_ref, st_ref, *, nsteps):
    t = pl.program_id(2)

    @pl.when(t == 0)
    def _():
        st_ref[...] = jnp.zeros_like(st_ref)

    u = u_ref[0]
    ts = u.shape[0]
    ub = u.astype(BF16)
    row = lax.broadcasted_iota(jnp.int32, u.shape, 0)
    first = row == 0
    y = u * d_ref[...]
    for q in range(4):
        xr = _dot(ub, b_ref[0, 2 * q])
        xi = _dot(ub, b_ref[0, 2 * q + 1])
        ar = a_ref[0, q, 0, 0:1, :]
        ai = a_ref[0, q, 1, 0:1, :]
        x0r = st_ref[2 * q:2 * q + 1, :]
        x0i = st_ref[2 * q + 1:2 * q + 2, :]
        xr = xr + jnp.where(first, ar * x0r - ai * x0i, 0.0)
        xi = xi + jnp.where(first, ar * x0i + ai * x0r, 0.0)
        for s in range(nsteps):
            k = 1 << s
            ar = a_ref[0, q, 0, s:s + 1, :]
            ai = a_ref[0, q, 1, s:s + 1, :]
            sr = _shift_rows(xr, k, row)
            si = _shift_rows(xi, k, row)
            xr, xi = xr + ar * sr - ai * si, xi + ar * si + ai * sr
        st_ref[2 * q:2 * q + 1, :] = xr[ts - 1:ts, :]
        st_ref[2 * q + 1:2 * q + 2, :] = xi[ts - 1:ts, :]
        y = y + _dot(xr.astype(BF16), c_ref[0, 2 * q]) + _dot(xi.astype(BF16), c_ref[0, 2 * q + 1])
    o_ref[0] = jax.nn.gelu(y).astype(o_ref.dtype)


def _s5_params(lam_re, lam_im, log_dt, b_re, b_im, c_re, c_im, nsteps):
    lr = lam_re.astype(F32)
    li = lam_im.astype(F32)
    dt = jnp.exp(log_dt.astype(F32))[:, None]
    mag = jnp.exp(lr * dt)
    ang = li * dt
    ab_re = mag * jnp.cos(ang)
    ab_im = mag * jnp.sin(ang)
    den = lr * lr + li * li
    cr = ((ab_re - 1.0) * lr + ab_im * li) / den
    ci = (ab_im * lr - (ab_re - 1.0) * li) / den
    br = b_re.astype(F32)
    bi = b_im.astype(F32)
    bb_re = cr[..., None] * br - ci[..., None] * bi
    bb_im = cr[..., None] * bi + ci[..., None] * br
    nblk = D_MODEL // LANES
    gpb = LANES // S5_GROUP
    npair = gpb // 2
    onehot = (jnp.arange(gpb)[None, None, :] == (2 * jnp.arange(npair)[:, None, None] + jnp.arange(2)[None, :, None]))
    onehot = onehot.astype(F32)

    def bsel(bb):
        bb = bb.reshape(nblk, npair, 2, S5_STATE, S5_GROUP)
        return jnp.einsum('kqmph,qmg->kqghmp', bb, onehot).reshape(nblk, npair, LANES, LANES)

    def csel(cc):
        cc = cc.astype(F32).reshape(nblk, npair, 2, S5_GROUP, S5_STATE)
        return jnp.einsum('kqmhp,qmg->kqmpgh', cc, onehot).reshape(nblk, npair, LANES, LANES)

    b_sel = jnp.stack([bsel(bb_re), bsel(bb_im)], axis=2).reshape(nblk, 2 * npair, LANES, LANES).astype(BF16)
    c_sel = jnp.stack([csel(c_re), -csel(c_im)], axis=2).reshape(nblk, 2 * npair, LANES, LANES).astype(BF16)
    ar = ab_re.reshape(-1, LANES)
    ai = ab_im.reshape(-1, LANES)
    pr, pi = [], []
    for _ in range(nsteps):
        pr.append(ar)
        pi.append(ai)
        ar, ai = ar * ar - ai * ai, 2.0 * ar * ai
    apow = jnp.stack([jnp.stack(pr, axis=1), jnp.stack(pi, axis=1)], axis=1)
    return b_sel, c_sel, apow.reshape(nblk, npair, 2, nsteps, LANES)


def _s5_core(u, b_sel, c_sel, apow, d_skip, ts):
    b, lp, d = u.shape
    nsteps = apow.shape[3]
    nblk = d // LANES
    return pl.pallas_call(
        functools.partial(_s5_kernel, nsteps=nsteps),
        grid=(b, nblk, lp // ts),
        in_specs=[pl.BlockSpec((1, ts, LANES), lambda bi, k, t: (bi, t, k)),
                  pl.BlockSpec((1, 8, LANES, LANES), lambda bi, k, t: (k, 0, 0, 0)),
                  pl.BlockSpec((1, 8, LANES, LANES), lambda bi, k, t: (k, 0, 0, 0)),
                  pl.BlockSpec((1, 4, 2, nsteps, LANES), lambda bi, k, t: (k, 0, 0, 0, 0)),
                  pl.BlockSpec((1, LANES), lambda bi, k, t: (0, k))],
        out_specs=pl.BlockSpec((1, ts, LANES), lambda bi, k, t: (bi, t, k)),
        out_shape=jax.ShapeDtypeStruct((b, lp, d), BF16),
        scratch_shapes=[pltpu.VMEM((8, LANES), F32)],
        compiler_params=_cparams("parallel", "parallel", "arbitrary"),
        name="s5_core",
    )(u, b_sel, c_sel, apow, d_skip.reshape(1, d))


def _s5_layer(h, gain, w_in, lam_re, lam_im, log_dt, b_re, b_im, c_re, c_im, d_skip, w_glu):
    b, lp, d = h.shape
    ts = SEQ_TILE
    b_sel, c_sel, apow = _s5_params(lam_re, lam_im, log_dt, b_re, b_im, c_re, c_im, int(math.log2(ts)))
    h2 = h.reshape(b * lp, d)
    u = _norm_proj(h2, gain, w_in.astype(BF16), F32).reshape(b, lp, d)
    y = _s5_core(u, b_sel, c_sel, apow, d_skip.astype(F32), ts)
    return _glu_out_proj(h2, y.reshape(b * lp, d), w_glu.astype(BF16)).reshape(b, lp, d)


def _rg_kernel(gate_ref, xb_ref, xbp_ref, cw_ref, cb_ref, wa_ref, ba_ref, wi_ref, bi_ref, lam_ref, o_ref,
               xc_ref, xcb_ref, st_ref, *, nsteps, halo):
    i = pl.program_id(1)

    @pl.when(i == 0)
    def _():
        st_ref[...] = jnp.zeros_like(st_ref)

    tr, width = xc_ref.shape
    keep = (i > 0).astype(F32)
    for c, w in _chunks(width, LANES):
        xx = jnp.concatenate([xbp_ref[0, :, c:c + w] * keep, xb_ref[0, :, c:c + w]], axis=0)
        conv = cb_ref[:, c:c + w] + cw_ref[3:4, c:c + w] * xx[halo:]
        for tap in range(1, 4):
            conv = conv + cw_ref[3 - tap:4 - tap, c:c + w] * pltpu.roll(xx, tap, 0)[halo:]
        xc_ref[:, c:c + w] = conv
        xcb_ref[:, c:c + w] = conv.astype(BF16)

    row = lax.broadcasted_iota(jnp.int32, (tr, 2 * LANES), 0)
    for c, w in _chunks(width, 2 * LANES):
        xcb = xcb_ref[...]
        r = jax.nn.sigmoid(_dot(xcb, wa_ref[:, c:c + w]) + ba_ref[:, c:c + w])
        ig = jax.nn.sigmoid(_dot(xcb, wi_ref[:, c:c + w]) + bi_ref[:, c:c + w])
        lam = lam_ref[:, c:c + w]
        softplus_neg_lam = jnp.maximum(-lam, 0.0) + jnp.log1p(jnp.exp(-jnp.abs(lam)))
        log_a = (-RG_C * r) * softplus_neg_lam
        a = jnp.exp(log_a)
        bv = jnp.sqrt(-jnp.tanh(log_a) * (a * a + 1.0)) * (ig * xc_ref[:, c:c + w])
        rw = row[:, :w]
        for s in range(nsteps):
            k = 1 << s
            bv = bv + a * _shift_rows(bv, k, rw)
            a = a * _shift_rows(a, k, rw, fill=1.0)
        hh = bv + a * st_ref[:, c:c + w]
        st_ref[:, c:c + w] = hh[tr - 1:tr, :]
        o_ref[0, :, c:c + w] = (hh * jax.nn.gelu(gate_ref[0, :, c:c + w])).astype(o_ref.dtype)


def _rg_core(gx, conv_w, conv_b, w_a, b_a, w_i, b_i, lam, tr):
    b, lp, w2 = gx.shape
    width = w2 // 2
    nl = width // LANES
    halo = SUBLANES
    hb = tr // halo
    vec = lambda v: v.reshape(1, width)
    return pl.pallas_call(
        functools.partial(_rg_kernel, nsteps=int(math.log2(tr)), halo=halo),
        grid=(b, lp // tr),
        in_specs=[pl.BlockSpec((1, tr, width), lambda bi, i: (bi, i, 0)),
                  pl.BlockSpec((1, tr, width), lambda bi, i: (bi, i, 1)),
                  pl.BlockSpec((1, halo, width), lambda bi, i: (bi, jnp.maximum(i * hb - 1, 0), 1)),
                  _const_spec((4, width)), _const_spec((1, width)),
                  _const_spec((width, width)), _const_spec((1, width)),
                  _const_spec((width, width)), _const_spec((1, width)), _const_spec((1, width))],
        out_specs=pl.BlockSpec((1, tr, width), lambda bi, i: (bi, i, 0)),
        out_shape=jax.ShapeDtypeStruct((b, lp, width), BF16),
        scratch_shapes=[pltpu.VMEM((tr, width), F32), pltpu.VMEM((tr, width), BF16), pltpu.VMEM((1, width), F32)],
        compiler_params=_cparams("parallel", "arbitrary"),
        name="rg_core",
    )(gx, gx, gx, conv_w, vec(conv_b), w_a, vec(b_a), w_i, vec(b_i), vec(lam))


def _block_diag(w):
    n, j, k = w.shape
    return jnp.einsum('njk,nm->njmk', w, jnp.eye(n, dtype=w.dtype)).reshape(n * j, n * k)


def _rg_layer(h, gain, w_in, conv_w, conv_b, w_a, b_a, w_i, b_i, lam, w_out):
    b, lp, d = h.shape
    width = w_out.shape[0]
    wp = -(-width // LANES) * LANES
    pad = wp - width
    padc = lambda v: jnp.pad(v.astype(F32), [(0, 0)] * (v.ndim - 1) + [(0, pad)])
    w_in_p = jnp.concatenate([padc(w_in[:, :width]), padc(w_in[:, width:])], axis=1).astype(BF16)
    dense = lambda w: jnp.pad(_block_diag(w.astype(F32)), ((0, pad), (0, pad))).astype(BF16)
    h2 = h.reshape(b * lp, d)
    gx = _norm_proj(h2, gain, w_in_p, F32).reshape(b, lp, 2 * wp)
    y = _rg_core(gx, padc(conv_w), padc(conv_b), dense(w_a), padc(b_a), dense(w_i), padc(b_i), padc(lam), SEQ_TILE)
    w_out_p = jnp.pad(w_out.astype(F32), ((0, pad), (0, 0))).astype(BF16)
    return _out_proj(h2, y.reshape(b * lp, wp), w_out_p).reshape(b, lp, d)


def _sb_kernel(q_ref, k_ref, v_ref, tri_ref, o_ref):
    i = pl.program_id(2)
    q = q_ref[0]
    tq = q.shape[0]
    lane = lax.broadcasted_iota(jnp.int32, (1, LANES), 1)
    rowc = lax.broadcasted_iota(jnp.int32, (tq, tq), 0)
    colc = lax.broadcasted_iota(jnp.int32, (tq, tq), 1)
    past = colc < rowc
    tri = tri_ref[...]
    out = jnp.zeros((tq, LANES), F32)
    for hh in range(LANES // SB_HEAD_DIM):
        head = jnp.logical_and(lane >= hh * SB_HEAD_DIM, lane < (hh + 1) * SB_HEAD_DIM)
        qh = jnp.where(head, q, jnp.zeros_like(q))

        def tile(j, carry, masked, qh=qh, head=head):
            acc, later = carry
            start = pl.multiple_of(j * tq, tq)
            z = _dot_nt(qh, k_ref[0, pl.ds(start, tq), :])
            soft = jnp.log(1.0 + jnp.exp(-jnp.abs(z)))
            log_beta = jnp.minimum(z, 0.0) - soft
            log_keep = -jnp.maximum(z, 0.0) - soft
            if masked:
                log_keep = jnp.where(past, log_keep, 0.0)
            hi = log_keep.astype(BF16)
            lo = (log_keep - hi.astype(F32)).astype(BF16)
            between = _dot(hi, tri) + _dot(lo, tri)
            w = jnp.exp(log_beta + between + later)
            if masked:
                w = jnp.where(past, w, 0.0)
            vt = v_ref[0, pl.ds(start, tq), :]
            vt = jnp.where(head, vt, jnp.zeros_like(vt))
            acc = acc + _dot(w.astype(BF16), vt)
            later = later + jnp.sum(log_keep, axis=-1, keepdims=True)
            return acc, later

        carry = tile(i, (jnp.zeros((tq, LANES), F32), jnp.zeros((tq, 1), F32)), True)
        carry = lax.fori_loop(0, i, lambda s, c: tile(i - 1 - s, c, False), carry)
        out = out + carry[0]
    o_ref[0] = out.astype(o_ref.dtype)


def _sb_core(qkv, tq):
    b, lp, d3 = qkv.shape
    d = d3 // 3
    npair = d // LANES
    idx = jnp.arange(tq)
    tri = (idx[:, None] > idx[None, :]).astype(BF16)
    return pl.pallas_call(
        _sb_kernel,
        grid=(b, npair, lp // tq),
        in_specs=[pl.BlockSpec((1, tq, LANES), lambda bi, p, i: (bi, i, p)),
                  pl.BlockSpec((1, lp, LANES), lambda bi, p, i: (bi, 0, npair + p)),
                  pl.BlockSpec((1, lp, LANES), lambda bi, p, i: (bi, 0, 2 * npair + p)),
                  _const_spec((tq, tq))],
        out_specs=pl.BlockSpec((1, tq, LANES), lambda bi, p, i: (bi, i, p)),
        out_shape=jax.ShapeDtypeStruct((b, lp, d), BF16),
        compiler_params=_cparams("parallel", "parallel", "arbitrary"),
        name="sb_core",
    )(qkv, qkv, qkv, tri)


def _sb_layer(h, gain, w_qkv, q_norm, k_norm, w_out):
    b, lp, d = h.shape
    h2 = h.reshape(b * lp, d)
    qkv = _qkv_proj(h2, gain, w_qkv.astype(BF16), q_norm, k_norm).reshape(b, lp, 3 * d)
    o = _sb_core(qkv, SEQ_TILE)
    return _out_proj(h2, o.reshape(b * lp, d), w_out.astype(BF16)).reshape(b, lp, d)


def _hg_kernel(q_ref, f_ref, v_ref, g_ref, gam_ref, on_ref, o_ref, st_ref, *, layer):
    t = pl.program_id(1)

    @pl.when(t == 0)
    def _():
        st_ref[...] = jnp.zeros_like(st_ref)

    gam = gam_ref[...]
    e = jnp.exp(gam - jnp.max(gam, axis=0, keepdims=True))
    lb = jnp.sum(e[:layer], axis=0, keepdims=True) / jnp.sum(e, axis=0, keepdims=True)
    fz = f_ref[0]
    th, d = fz.shape
    logf = jnp.log(lb + (1.0 - lb) * jax.nn.sigmoid(fz))
    kk = (1.0 - lb) * jax.nn.sigmoid(-fz)
    row = lax.broadcasted_iota(jnp.int32, (th, d), 0)
    cum = logf
    for s in range(int(math.log2(th))):
        cum = cum + _shift_rows(cum, 1 << s, row)
    total = cum[th - 1:th, :]
    qv = q_ref[0]
    vv = v_ref[0]
    rowc = lax.broadcasted_iota(jnp.int32, (th, th), 0)
    colc = lax.broadcasted_iota(jnp.int32, (th, th), 1)

    levels = []
    w = th // 2
    while w >= HG_BLOCK:
        blk = 2 * w
        ref = jnp.concatenate([jnp.broadcast_to(cum[s0 + w - 1:s0 + w, :], (blk, d)) for s0 in range(0, th, blk)],
                              axis=0)
        qw = (qv * jnp.exp(jnp.minimum(cum - ref, 0.0))).astype(BF16)
        kw = (kk * jnp.exp(jnp.minimum(ref - cum, 0.0))).astype(BF16)
        sh = int(math.log2(blk))
        mask = jnp.logical_and((rowc >> sh) == (colc >> sh),
                               jnp.logical_and((rowc & (blk - 1)) >= w, (colc & (blk - 1)) < w))
        levels.append((qw, kw, mask))
        w //= 2
    ref = jnp.concatenate([jnp.zeros((HG_BLOCK, d), F32)]
                          + [jnp.broadcast_to(cum[s0 - 1:s0, :], (HG_BLOCK, d)) for s0 in range(HG_BLOCK, th, HG_BLOCK)],
                          axis=0)
    qd = (qv * jnp.exp(cum - ref)).astype(BF16)
    kd = (kk * jnp.exp(ref - cum)).astype(BF16)
    sh = int(math.log2(HG_BLOCK))
    levels.append((qd, kd, jnp.logical_and((rowc >> sh) == (colc >> sh), colc <= rowc)))
    q_in = (qv * jnp.exp(cum)).astype(BF16)
    k_out = (kk * jnp.exp(total - cum)).astype(BF16)
    dec = jnp.exp(total)

    for hd in range(d // HG_HEAD_DIM):
        sl = slice(hd * HG_HEAD_DIM, (hd + 1) * HG_HEAD_DIM)
        scores = jnp.zeros((th, th), F32)
        for qw, kw, mask in levels:
            scores = scores + jnp.where(mask, _dot_nt(qw[:, sl], kw[:, sl]), 0.0)
        vh = vv[:, sl].astype(BF16)
        st = st_ref[hd]
        o = _dot(scores.astype(BF16), vh) + _dot_nt(q_in[:, sl], st.astype(BF16))
        st_ref[hd] = st * dec[:, sl] + _dot_tn(vh, k_out[:, sl])
        on = o * lax.rsqrt(jnp.mean(o * o, axis=-1, keepdims=True) + NORM_EPS) * on_ref[:, sl]
        gg = g_ref[0, :, sl]
        o_ref[0, :, sl] = (on * (gg * jax.nn.sigmoid(gg))).astype(o_ref.dtype)


def _hg_core(qfig, gamma, layer, o_norm, th):
    b, lp, d4 = qfig.shape
    d = d4 // 4
    depth = gamma.shape[0]
    spec = lambda n: pl.BlockSpec((1, th, d), lambda bi, t: (bi, t, n))
    return pl.pallas_call(
        functools.partial(_hg_kernel, layer=layer),
        grid=(b, lp // th),
        in_specs=[spec(0), spec(1), spec(2), spec(3), _const_spec((depth, d)), _const_spec((1, d))],
        out_specs=pl.BlockSpec((1, th, d), lambda bi, t: (bi, t, 0)),
        out_shape=jax.ShapeDtypeStruct((b, lp, d), BF16),
        scratch_shapes=[pltpu.VMEM((d // HG_HEAD_DIM, HG_HEAD_DIM, HG_HEAD_DIM), F32)],
        compiler_params=_cparams("parallel", "arbitrary"),
        name="hg_core",
    )(qfig, qfig, qfig, qfig, gamma.astype(F32), o_norm.astype(F32).reshape(1, d))


def _hg_layer(h, gain, w_in, gamma, layer, o_norm, w_out):
    b, lp, d = h.shape
    h2 = h.reshape(b * lp, d)
    qfig = _norm_proj(h2, gain, w_in.astype(BF16), F32).reshape(b, lp, 4 * d)
    y = _hg_core(qfig, gamma, layer, o_norm, LANES)
    return _out_proj(h2, y.reshape(b * lp, d), w_out.astype(BF16)).reshape(b, lp, d)


def kernel(x, meta_tokens, norm_mix, norm_ffn, s5_w_in, s5_lam_re, s5_lam_im, s5_log_dt, s5_b_re, s5_b_im, s5_c_re, s5_c_im, s5_d, s5_w_glu, rg_w_in, rg_conv_w, rg_conv_b, rg_w_a, rg_b_a, rg_w_i, rg_b_i, rg_lambda, rg_w_out, sb_w_qkv, sb_q_norm, sb_k_norm, sb_w_out, hg_w_in, hg_gamma, hg_o_norm, hg_w_out, ffn_w_up, ffn_conv_w, ffn_conv_b, ffn_w_down):
    bsz, seq, d = x.shape
    n_meta = meta_tokens.shape[0]
    length = n_meta + seq
    lp = -(-length // SEQ_TILE) * SEQ_TILE
    meta = jnp.broadcast_to(meta_tokens.astype(x.dtype)[None], (bsz, n_meta, d))
    h = jnp.concatenate([meta, x, jnp.zeros((bsz, lp - length, d), x.dtype)], axis=1)
    depth = norm_mix.shape[0]
    n_mixers = 4
    for layer in range(depth):
        m, j = layer % n_mixers, layer // n_mixers
        g = norm_mix[layer].astype(F32)
        if m == 0:
            h = _s5_layer(h, g, s5_w_in[j], s5_lam_re[j], s5_lam_im[j], s5_log_dt[j], s5_b_re[j], s5_b_im[j],
                          s5_c_re[j], s5_c_im[j], s5_d[j], s5_w_glu[j])
        elif m == 1:
            h = _rg_layer(h, g, rg_w_in[j], rg_conv_w[j], rg_conv_b[j], rg_w_a[j], rg_b_a[j], rg_w_i[j], rg_b_i[j],
                          rg_lambda[j], rg_w_out[j])
        elif m == 2:
            h = _sb_layer(h, g, sb_w_qkv[j], sb_q_norm[j], sb_k_norm[j], sb_w_out[j])
        else:
            h = _hg_layer(h, g, hg_w_in[j], hg_gamma, layer, hg_o_norm[j], hg_w_out[j])
        h = _conv_ffn(h, norm_ffn[layer].astype(F32), ffn_w_up[layer].astype(BF16), ffn_conv_w[layer].astype(F32),
                      ffn_conv_b[layer].astype(F32), ffn_w_down[layer].astype(BF16))
    return h[:, n_meta:length]
```

```python
import functools
import math

import jax
import jax.numpy as jnp
from jax import lax
from jax.experimental import pallas as pl
from jax.experimental.pallas import tpu as pltpu

F32 = jnp.float32
BF16 = jnp.bfloat16

D_MODEL = 1024
N_META = 16
NORM_EPS = 1e-6
LANES = 128
SUBLANES = 8
SEQ_TILE = 256
ROW_TILE = 512
VMEM_LIMIT_BYTES = 56 * 1024 * 1024

FFN_CONV = 3
S5_GROUP = 16
S5_STATE = 64
RG_BLOCKS = 16
RG_C = 8.0
SB_HEAD_DIM = 64
HG_HEAD_DIM = 128
HG_HEADS = D_MODEL // HG_HEAD_DIM
HG_BLOCK = 16


def _cparams(*sem):
    return pltpu.CompilerParams(dimension_semantics=sem, vmem_limit_bytes=VMEM_LIMIT_BYTES)


def _const_spec(shape):
    nd = len(shape)
    return pl.BlockSpec(shape, lambda *_: (0,) * nd)


def _rms(x, g):
    return x * lax.rsqrt(jnp.mean(x * x, axis=-1, keepdims=True) + NORM_EPS) * g


def _dot(a, b):
    return jnp.dot(a, b, preferred_element_type=F32)


def _dot_nt(a, b):
    return lax.dot_general(a, b, (((1,), (1,)), ((), ())), preferred_element_type=F32)


def _dot_tn(a, b):
    return lax.dot_general(a, b, (((0,), (0,)), ((), ())), preferred_element_type=F32)


def _shift_rows(x, k, row, fill=0.0):
    n = x.shape[0]
    if k % SUBLANES == 0:
        return jnp.concatenate([jnp.full((k,) + x.shape[1:], fill, x.dtype), x[: n - k]], axis=0)
    return jnp.where(row >= k, pltpu.roll(x, k, 0), fill)


def _chunks(n, width):
    out, c = [], 0
    while c < n:
        w = min(width, n - c)
        out.append((c, w))
        c += w
    return out


def _proj_kernel(x_ref, g_ref, w_ref, o_ref):
    xn = _rms(x_ref[...], g_ref[...]).astype(BF16)
    for c, w in _chunks(o_ref.shape[-1], 512):
        o_ref[:, c:c + w] = _dot(xn, w_ref[:, c:c + w]).astype(o_ref.dtype)


def _norm_proj(h2, gain, w, out_dtype):
    m, d = h2.shape
    n = w.shape[1]
    tm = ROW_TILE if m % ROW_TILE == 0 else SEQ_TILE
    return pl.pallas_call(
        _proj_kernel,
        grid=(m // tm,),
        in_specs=[pl.BlockSpec((tm, d), lambda i: (i, 0)), _const_spec((1, d)), _const_spec((d, n))],
        out_specs=pl.BlockSpec((tm, n), lambda i: (i, 0)),
        out_shape=jax.ShapeDtypeStruct((m, n), out_dtype),
        compiler_params=_cparams("parallel"),
        name="norm_proj",
    )(h2, gain.reshape(1, d), w)


def _qkv_kernel(x_ref, g_ref, w_ref, avg_ref, qg_ref, kg_ref, o_ref):
    xn = _rms(x_ref[...], g_ref[...]).astype(BF16)
    d = x_ref.shape[-1]
    for c, w in _chunks(o_ref.shape[-1], 512):
        r = _dot(xn, w_ref[:, c:c + w])
        if c < 2 * d:
            ms = _dot((r * r).astype(BF16), avg_ref[...])
            gain = qg_ref[...] if c < d else kg_ref[...]
            r = r * lax.rsqrt(ms + NORM_EPS) * gain
        o_ref[:, c:c + w] = r.astype(o_ref.dtype)


def _qkv_proj(h2, gain, w, q_norm, k_norm):
    m, d = h2.shape
    n = w.shape[1]
    tm = ROW_TILE if m % ROW_TILE == 0 else SEQ_TILE
    heads_per_chunk = 512 // SB_HEAD_DIM
    blk = jnp.arange(512) // SB_HEAD_DIM
    avg = jnp.where(blk[:, None] == blk[None, :], 1.0 / SB_HEAD_DIM, 0.0).astype(BF16)
    scale = SB_HEAD_DIM ** -0.5
    qg = jnp.tile(q_norm.astype(F32) * scale, heads_per_chunk).reshape(1, 512)
    kg = jnp.tile(k_norm.astype(F32), heads_per_chunk).reshape(1, 512)
    return pl.pallas_call(
        _qkv_kernel,
        grid=(m // tm,),
        in_specs=[pl.BlockSpec((tm, d), lambda i: (i, 0)), _const_spec((1, d)), _const_spec((d, n)),
                  _const_spec((512, 512)), _const_spec((1, 512)), _const_spec((1, 512))],
        out_specs=pl.BlockSpec((tm, n), lambda i: (i, 0)),
        out_shape=jax.ShapeDtypeStruct((m, n), BF16),
        compiler_params=_cparams("parallel"),
        name="qkv_proj",
    )(h2, gain.reshape(1, d), w, avg, qg, kg)


def _out_kernel(h_ref, y_ref, w_ref, o_ref):
    y = y_ref[...]
    for c, w in _chunks(o_ref.shape[-1], 512):
        o_ref[:, c:c + w] = h_ref[:, c:c + w] + _dot(y, w_ref[:, c:c + w])


def _out_proj(h2, y2, w):
    m, d = h2.shape
    k = y2.shape[1]
    tm = ROW_TILE if m % ROW_TILE == 0 else SEQ_TILE
    return pl.pallas_call(
        _out_kernel,
        grid=(m // tm,),
        in_specs=[pl.BlockSpec((tm, d), lambda i: (i, 0)), pl.BlockSpec((tm, k), lambda i: (i, 0)),
                  _const_spec((k, d))],
        out_specs=pl.BlockSpec((tm, d), lambda i: (i, 0)),
        out_shape=jax.ShapeDtypeStruct((m, d), F32),
        compiler_params=_cparams("parallel"),
        name="out_proj",
    )(h2, y2, w)


def _glu_out_kernel(h_ref, y_ref, w_ref, o_ref):
    y = y_ref[...]
    d = o_ref.shape[-1]
    for c, w in _chunks(d, 512):
        a = _dot(y, w_ref[:, c:c + w])
        g = _dot(y, w_ref[:, d + c:d + c + w])
        o_ref[:, c:c + w] = h_ref[:, c:c + w] + a * jax.nn.sigmoid(g)


def _glu_out_proj(h2, y2, w):
    m, d = h2.shape
    k = y2.shape[1]
    tm = ROW_TILE if m % ROW_TILE == 0 else SEQ_TILE
    return pl.pallas_call(
        _glu_out_kernel,
        grid=(m // tm,),
        in_specs=[pl.BlockSpec((tm, d), lambda i: (i, 0)), pl.BlockSpec((tm, k), lambda i: (i, 0)),
                  _const_spec((k, 2 * d))],
        out_specs=pl.BlockSpec((tm, d), lambda i: (i, 0)),
        out_shape=jax.ShapeDtypeStruct((m, d), F32),
        compiler_params=_cparams("parallel"),
        name="glu_out_proj",
    )(h2, y2, w)


def _ffn_kernel(h_ref, hp_ref, g_ref, wu_ref, cw_ref, cb_ref, wd_ref, o_ref, gated_ref, *, d_ff, halo, seq_len):
    i = pl.program_id(0)
    xc = h_ref[...]
    tm, d = xc.shape
    x = jnp.concatenate([hp_ref[...], xc], axis=0)
    row = lax.broadcasted_iota(jnp.int32, (halo + tm, 1), 0)
    pos = lax.rem(i * tm - halo + seq_len, seq_len) + row
    pos = jnp.where(pos >= seq_len, pos - seq_len, pos)
    xn = jnp.where(pos < seq_len - (FFN_CONV - 1), _rms(x, g_ref[...]), 0.0).astype(BF16)

    def conv(u, c, w):
        u1 = pltpu.roll(u, 1, 0)
        u2 = pltpu.roll(u, 2, 0)
        return (cb_ref[:, c:c + w] + cw_ref[2:3, c:c + w] * u[halo:] + cw_ref[1:2, c:c + w] * u1[halo:]
                + cw_ref[0:1, c:c + w] * u2[halo:])

    for c, w in _chunks(d_ff, 512):
        a = conv(_dot(xn, wu_ref[:, c:c + w]), c, w)
        b = conv(_dot(xn, wu_ref[:, d_ff + c:d_ff + c + w]), d_ff + c, w)
        gated_ref[:, c:c + w] = (a * jax.nn.sigmoid(a) * b).astype(BF16)
    gated = gated_ref[...]
    for c, w in _chunks(d, 512):
        o_ref[:, c:c + w] = xc[:, c:c + w] + _dot(gated, wd_ref[:, c:c + w])


def _conv_ffn(h, gain, w_up, conv_w, conv_b, w_down):
    b, lp, d = h.shape
    d_ff = w_down.shape[0]
    m = b * lp
    tm = ROW_TILE if m % ROW_TILE == 0 else SEQ_TILE
    halo = SUBLANES
    hb = tm // halo
    out = pl.pallas_call(
        functools.partial(_ffn_kernel, d_ff=d_ff, halo=halo, seq_len=lp),
        grid=(m // tm,),
        in_specs=[pl.BlockSpec((tm, d), lambda i: (i, 0)),
                  pl.BlockSpec((halo, d), lambda i: (jnp.maximum(i * hb - 1, 0), 0)),
                  _const_spec((1, d)), _const_spec((d, 2 * d_ff)), _const_spec((3, 2 * d_ff)),
                  _const_spec((1, 2 * d_ff)), _const_spec((d_ff, d))],
        out_specs=pl.BlockSpec((tm, d), lambda i: (i, 0)),
        out_shape=jax.ShapeDtypeStruct((m, d), F32),
        scratch_shapes=[pltpu.VMEM((tm, d_ff), BF16)],
        compiler_params=_cparams("parallel"),
        name="conv_ffn",
    )(h.reshape(m, d), h.reshape(m, d), gain.reshape(1, d), w_up, conv_w, conv_b.reshape(1, -1), w_down)
    return out.reshape(b, lp, d)


def _s5_kernel(u_ref, b_ref, c_ref, a_ref, d_ref, o_ref, st_ref, *, nsteps):
    t = pl.program_id(2)

    @pl.when(t == 0)
    def _():
        st_ref[...] = jnp.zeros_like(st_ref)

    u = u_ref[0]
    ts = u.shape[0]
    ub = u.astype(BF16)
    row = lax.broadcasted_iota(jnp.int32, u.shape, 0)
    first = row == 0
    y = u * d_ref[...]
    for q in range(4):
        xr = _dot(ub, b_ref[0, 2 * q])
        xi = _dot(ub, b_ref[0, 2 * q + 1])
        ar = a_ref[0, q, 0, 0:1, :]
        ai = a_ref[0, q, 1, 0:1, :]
        x0r = st_ref[2 * q:2 * q + 1, :]
        x0i = st_ref[2 * q + 1:2 * q + 2, :]
        xr = xr + jnp.where(first, ar * x0r - ai * x0i, 0.0)
        xi = xi + jnp.where(first, ar * x0i + ai * x0r, 0.0)
        for s in range(nsteps):
            k = 1 << s
            ar = a_ref[0, q, 0, s:s + 1, :]
            ai = a_ref[0, q, 1, s:s + 1, :]
            sr = _shift_rows(xr, k, row)
            si = _shift_rows(xi, k, row)
            xr, xi = xr + ar * sr - ai * si, xi + ar * si + ai * sr
        st_ref[2 * q:2 * q + 1, :] = xr[ts - 1:ts, :]
        st_ref[2 * q + 1:2 * q + 2, :] = xi[ts - 1:ts, :]
        y = y + _dot(xr.astype(BF16), c_ref[0, 2 * q]) + _dot(xi.astype(BF16), c_ref[0, 2 * q + 1])
    o_ref[0] = jax.nn.gelu(y).astype(o_ref.dtype)


def _s5_params(lam_re, lam_im, log_dt, b_re, b_im, c_re, c_im, nsteps):
    lr = lam_re.astype(F32)
    li = lam_im.astype(F32)
    dt = jnp.exp(log_dt.astype(F32))[:, None]
    mag = jnp.exp(lr * dt)
    ang = li * dt
    ab_re = mag * jnp.cos(ang)
    ab_im = mag * jnp.sin(ang)
    den = lr * lr + li * li
    cr = ((ab_re - 1.0) * lr + ab_im * li) / den
    ci = (ab_im * lr - (ab_re - 1.0) * li) / den
    br = b_re.astype(F32)
    bi = b_im.astype(F32)
    bb_re = cr[..., None] * br - ci[..., None] * bi
    bb_im = cr[..., None] * bi + ci[..., None] * br
    nblk = D_MODEL // LANES
    gpb = LANES // S5_GROUP
    npair = gpb // 2
    onehot = (jnp.arange(gpb)[None, None, :] == (2 * jnp.arange(npair)[:, None, None] + jnp.arange(2)[None, :, None]))
    onehot = onehot.astype(F32)

    def bsel(bb):
        bb = bb.reshape(nblk, npair, 2, S5_STATE, S5_GROUP)
        return jnp.einsum('kqmph,qmg->kqghmp', bb, onehot).reshape(nblk, npair, LANES, LANES)

    def csel(cc):
        cc = cc.astype(F32).reshape(nblk, npair, 2, S5_GROUP, S5_STATE)
        return jnp.einsum('kqmhp,qmg->kqmpgh', cc, onehot).reshape(nblk, npair, LANES, LANES)

    b_sel = jnp.stack([bsel(bb_re), bsel(bb_im)], axis=2).reshape(nblk, 2 * npair, LANES, LANES).astype(BF16)
    c_sel = jnp.stack([csel(c_re), -csel(c_im)], axis=2).reshape(nblk, 2 * npair, LANES, LANES).astype(BF16)
    ar = ab_re.reshape(-1, LANES)
    ai = ab_im.reshape(-1, LANES)
    pr, pi = [], []
    for _ in range(nsteps):
        pr.append(ar)
        pi.append(ai)
        ar, ai = ar * ar - ai * ai, 2.0 * ar * ai
    apow = jnp.stack([jnp.stack(pr, axis=1), jnp.stack(pi, axis=1)], axis=1)
    return b_sel, c_sel, apow.reshape(nblk, npair, 2, nsteps, LANES)


def _s5_core(u, b_sel, c_sel, apow, d_skip, ts):
    b, lp, d = u.shape
    nsteps = apow.shape[3]
    nblk = d // LANES
    return pl.pallas_call(
        functools.partial(_s5_kernel, nsteps=nsteps),
        grid=(b, nblk, lp // ts),
        in_specs=[pl.BlockSpec((1, ts, LANES), lambda bi, k, t: (bi, t, k)),
                  pl.BlockSpec((1, 8, LANES, LANES), lambda bi, k, t: (k, 0, 0, 0)),
                  pl.BlockSpec((1, 8, LANES, LANES), lambda bi, k, t: (k, 0, 0, 0)),
                  pl.BlockSpec((1, 4, 2, nsteps, LANES), lambda bi, k, t: (k, 0, 0, 0, 0)),
                  pl.BlockSpec((1, LANES), lambda bi, k, t: (0, k))],
        out_specs=pl.BlockSpec((1, ts, LANES), lambda bi, k, t: (bi, t, k)),
        out_shape=jax.ShapeDtypeStruct((b, lp, d), BF16),
        scratch_shapes=[pltpu.VMEM((8, LANES), F32)],
        compiler_params=_cparams("parallel", "parallel", "arbitrary"),
        name="s5_core",
    )(u, b_sel, c_sel, apow, d_skip.reshape(1, d))


def _s5_layer(h, gain, w_in, lam_re, lam_im, log_dt, b_re, b_im, c_re, c_im, d_skip, w_glu):
    b, lp, d = h.shape
    ts = SEQ_TILE
    b_sel, c_sel, apow = _s5_params(lam_re, lam_im, log_dt, b_re, b_im, c_re, c_im, int(math.log2(ts)))
    h2 = h.reshape(b * lp, d)
    u = _norm_proj(h2, gain, w_in.astype(BF16), F32).reshape(b, lp, d)
    y = _s5_core(u, b_sel, c_sel, apow, d_skip.astype(F32), ts)
    return _glu_out_proj(h2, y.reshape(b * lp, d), w_glu.astype(BF16)).reshape(b, lp, d)


def _rg_kernel(gate_ref, xb_ref, xbp_ref, cw_ref, cb_ref, wa_ref, ba_ref, wi_ref, bi_ref, lam_ref, o_ref,
               xc_ref, xcb_ref, st_ref, *, nsteps, halo):
    i = pl.program_id(1)

    @pl.when(i == 0)
    def _():
        st_ref[...] = jnp.zeros_like(st_ref)

    tr, width = xc_ref.shape
    keep = (i > 0).astype(F32)
    for c, w in _chunks(width, LANES):
        xx = jnp.concatenate([xbp_ref[0, :, c:c + w] * keep, xb_ref[0, :, c:c + w]], axis=0)
        conv = cb_ref[:, c:c + w] + cw_ref[3:4, c:c + w] * xx[halo:]
        for tap in range(1, 4):
            conv = conv + cw_ref[3 - tap:4 - tap, c:c + w] * pltpu.roll(xx, tap, 0)[halo:]
        xc_ref[:, c:c + w] = conv
        xcb_ref[:, c:c + w] = conv.astype(BF16)

    row = lax.broadcasted_iota(jnp.int32, (tr, 2 * LANES), 0)
    for c, w in _chunks(width, 2 * LANES):
        xcb = xcb_ref[...]
        r = jax.nn.sigmoid(_dot(xcb, wa_ref[:, c:c + w]) + ba_ref[:, c:c + w])
        ig = jax.nn.sigmoid(_dot(xcb, wi_ref[:, c:c + w]) + bi_ref[:, c:c + w])
        lam = lam_ref[:, c:c + w]
        softplus_neg_lam = jnp.maximum(-lam, 0.0) + jnp.log1p(jnp.exp(-jnp.abs(lam)))
        log_a = (-RG_C * r) * softplus_neg_lam
        a = jnp.exp(log_a)
        bv = jnp.sqrt(-jnp.tanh(log_a) * (a * a + 1.0)) * (ig * xc_ref[:, c:c + w])
        rw = row[:, :w]
        for s in range(nsteps):
            k = 1 << s
            bv = bv + a * _shift_rows(bv, k, rw)
            a = a * _shift_rows(a, k, rw, fill=1.0)
        hh = bv + a * st_ref[:, c:c + w]
        st_ref[:, c:c + w] = hh[tr - 1:tr, :]
        o_ref[0, :, c:c + w] = (hh * jax.nn.gelu(gate_ref[0, :, c:c + w])).astype(o_ref.dtype)


def _rg_core(gx, conv_w, conv_b, w_a, b_a, w_i, b_i, lam, tr):
    b, lp, w2 = gx.shape
    width = w2 // 2
    nl = width // LANES
    halo = SUBLANES
    hb = tr // halo
    vec = lambda v: v.reshape(1, width)
    return pl.pallas_call(
        functools.partial(_rg_kernel, nsteps=int(math.log2(tr)), halo=halo),
        grid=(b, lp // tr),
        in_specs=[pl.BlockSpec((1, tr, width), lambda bi, i: (bi, i, 0)),
                  pl.BlockSpec((1, tr, width), lambda bi, i: (bi, i, 1)),
                  pl.BlockSpec((1, halo, width), lambda bi, i: (bi, jnp.maximum(i * hb - 1, 0), 1)),
                  _const_spec((4, width)), _const_spec((1, width)),
                  _const_spec((width, width)), _const_spec((1, width)),
                  _const_spec((width, width)), _const_spec((1, width)), _const_spec((1, width))],
        out_specs=pl.BlockSpec((1, tr, width), lambda bi, i: (bi, i, 0)),
        out_shape=jax.ShapeDtypeStruct((b, lp, width), BF16),
        scratch_shapes=[pltpu.VMEM((tr, width), F32), pltpu.VMEM((tr, width), BF16), pltpu.VMEM((1, width), F32)],
        compiler_params=_cparams("parallel", "arbitrary"),
        name="rg_core",
    )(gx, gx, gx, conv_w, vec(conv_b), w_a, vec(b_a), w_i, vec(b_i), vec(lam))


def _block_diag(w):
    n, j, k = w.shape
    return jnp.einsum('njk,nm->njmk', w, jnp.eye(n, dtype=w.dtype)).reshape(n * j, n * k)


def _rg_layer(h, gain, w_in, conv_w, conv_b, w_a, b_a, w_i, b_i, lam, w_out):
    b, lp, d = h.shape
    width = w_out.shape[0]
    wp = -(-width // LANES) * LANES
    pad = wp - width
    padc = lambda v: jnp.pad(v.astype(F32), [(0, 0)] * (v.ndim - 1) + [(0, pad)])
    w_in_p = jnp.concatenate([padc(w_in[:, :width]), padc(w_in[:, width:])], axis=1).astype(BF16)
    dense = lambda w: jnp.pad(_block_diag(w.astype(F32)), ((0, pad), (0, pad))).astype(BF16)
    h2 = h.reshape(b * lp, d)
    gx = _norm_proj(h2, gain, w_in_p, F32).reshape(b, lp, 2 * wp)
    y = _rg_core(gx, padc(conv_w), padc(conv_b), dense(w_a), padc(b_a), dense(w_i), padc(b_i), padc(lam), SEQ_TILE)
    w_out_p = jnp.pad(w_out.astype(F32), ((0, pad), (0, 0))).astype(BF16)
    return _out_proj(h2, y.reshape(b * lp, wp), w_out_p).reshape(b, lp, d)


def _sb_kernel(q_ref, k_ref, v_ref, tri_ref, o_ref, acc_ref, later_ref):
    i = pl.program_id(2)
    q = q_ref[0]
    tq = q.shape[0]
    lane = lax.broadcasted_iota(jnp.int32, (1, LANES), 1)
    head0 = lane < SB_HEAD_DIM
    zero = jnp.zeros_like(q)
    q2 = jnp.concatenate([jnp.where(head0, q, zero), jnp.where(head0, zero, q)], axis=0)
    rowc = lax.broadcasted_iota(jnp.int32, (2 * tq, tq), 0) & (tq - 1)
    colc = lax.broadcasted_iota(jnp.int32, (2 * tq, tq), 1)
    past = colc < rowc
    tri = tri_ref[...]
    acc_ref[...] = jnp.zeros_like(acc_ref)
    later_ref[...] = jnp.zeros_like(later_ref)

    def sub(j, later, mask):
        start = pl.multiple_of(j * tq, tq)
        z = _dot_nt(q2, k_ref[0, pl.ds(start, tq), :])
        neg_abs = lax.bitcast_convert_type(lax.bitcast_convert_type(z, jnp.uint32) | jnp.uint32(0x80000000), F32)
        log_beta = jnp.minimum(z, 0.0) - jnp.log(1.0 + jnp.exp(neg_abs))
        log_keep = log_beta - z
        if mask is not None:
            log_keep = jnp.where(mask, log_keep, 0.0)
        agg = _dot(log_keep.astype(BF16), tri)
        w = jnp.exp(log_beta + agg[:, :tq] + jnp.concatenate([later] * (tq // LANES), axis=1))
        if mask is not None:
            w = jnp.where(mask, w, 0.0)
        return _dot(w.astype(BF16), v_ref[0, pl.ds(start, tq), :]), agg[:, tq:]

    def pair(j, mask):
        later = later_ref[...]
        pv1, rs1 = sub(j, later, mask)
        pv0, rs0 = sub(j - 1, later + rs1, None)
        acc_ref[...] += pv1 + pv0
        later_ref[...] = later + (rs1 + rs0)

    @pl.when(i >= 1)
    def _():
        pair(i, past)

    def body(p, carry):
        pair(i - 2 * p, None)
        return carry

    lax.fori_loop(1, (i + 1) // 2, body, 0)

    @pl.when((i & 1) == 0)
    def _():
        pv, _ = sub(0, later_ref[...], jnp.logical_or(past, i > 0))
        acc_ref[...] += pv

    acc = acc_ref[...]
    o_ref[0] = jnp.where(head0, acc[:tq], acc[tq:]).astype(o_ref.dtype)


def _sb_core(qkv, tq):
    b, lp, d3 = qkv.shape
    d = d3 // 3
    npair = d // LANES
    idx = jnp.arange(tq)
    tri = jnp.concatenate([(idx[:, None] > idx[None, :]).astype(BF16), jnp.ones((tq, LANES), BF16)], axis=1)
    return pl.pallas_call(
        _sb_kernel,
        grid=(b, npair, lp // tq),
        in_specs=[pl.BlockSpec((1, tq, LANES), lambda bi, p, i: (bi, i, p)),
                  pl.BlockSpec((1, lp, LANES), lambda bi, p, i: (bi, 0, npair + p)),
                  pl.BlockSpec((1, lp, LANES), lambda bi, p, i: (bi, 0, 2 * npair + p)),
                  _const_spec((tq, tq + LANES))],
        out_specs=pl.BlockSpec((1, tq, LANES), lambda bi, p, i: (bi, i, p)),
        out_shape=jax.ShapeDtypeStruct((b, lp, d), BF16),
        scratch_shapes=[pltpu.VMEM((2 * tq, LANES), F32), pltpu.VMEM((2 * tq, LANES), F32)],
        compiler_params=_cparams("parallel", "parallel", "arbitrary"),
        name="sb_core",
    )(qkv, qkv, qkv, tri)


def _sb_layer(h, gain, w_qkv, q_norm, k_norm, w_out):
    b, lp, d = h.shape
    h2 = h.reshape(b * lp, d)
    qkv = _qkv_proj(h2, gain, w_qkv.astype(BF16), q_norm, k_norm).reshape(b, lp, 3 * d)
    o = _sb_core(qkv, SEQ_TILE)
    return _out_proj(h2, o.reshape(b * lp, d), w_out.astype(BF16)).reshape(b, lp, d)


def _hg_kernel(q_ref, f_ref, v_ref, g_ref, gam_ref, on_ref, o_ref, st_ref, *, layer):
    t = pl.program_id(1)

    @pl.when(t == 0)
    def _():
        st_ref[...] = jnp.zeros_like(st_ref)

    gam = gam_ref[...]
    e = jnp.exp(gam - jnp.max(gam, axis=0, keepdims=True))
    lb = jnp.sum(e[:layer], axis=0, keepdims=True) / jnp.sum(e, axis=0, keepdims=True)
    fz = f_ref[0]
    th, d = fz.shape
    logf = jnp.log(lb + (1.0 - lb) * jax.nn.sigmoid(fz))
    kk = (1.0 - lb) * jax.nn.sigmoid(-fz)
    row = lax.broadcasted_iota(jnp.int32, (th, d), 0)
    cum = logf
    for s in range(int(math.log2(th))):
        cum = cum + _shift_rows(cum, 1 << s, row)
    total = cum[th - 1:th, :]
    qv = q_ref[0]
    vv = v_ref[0]
    rowc = lax.broadcasted_iota(jnp.int32, (th, th), 0)
    colc = lax.broadcasted_iota(jnp.int32, (th, th), 1)

    levels = []
    w = th // 2
    while w >= HG_BLOCK:
        blk = 2 * w
        ref = jnp.concatenate([jnp.broadcast_to(cum[s0 + w - 1:s0 + w, :], (blk, d)) for s0 in range(0, th, blk)],
                              axis=0)
        qw = (qv * jnp.exp(jnp.minimum(cum - ref, 0.0))).astype(BF16)
        kw = (kk * jnp.exp(jnp.minimum(ref - cum, 0.0))).astype(BF16)
        sh = int(math.log2(blk))
        mask = jnp.logical_and((rowc >> sh) == (colc >> sh),
                               jnp.logical_and((rowc & (blk - 1)) >= w, (colc & (blk - 1)) < w))
        levels.append((qw, kw, mask))
        w //= 2
    ref = jnp.concatenate([jnp.zeros((HG_BLOCK, d), F32)]
                          + [jnp.broadcast_to(cum[s0 - 1:s0, :], (HG_BLOCK, d)) for s0 in range(HG_BLOCK, th, HG_BLOCK)],
                          axis=0)
    qd = (qv * jnp.exp(cum - ref)).astype(BF16)
    kd = (kk * jnp.exp(ref - cum)).astype(BF16)
    sh = int(math.log2(HG_BLOCK))
    levels.append((qd, kd, jnp.logical_and((rowc >> sh) == (colc >> sh), colc <= rowc)))
    q_in = (qv * jnp.exp(cum)).astype(BF16)
    k_out = (kk * jnp.exp(total - cum)).astype(BF16)
    dec = jnp.exp(total)

    for hd in range(d // HG_HEAD_DIM):
        sl = slice(hd * HG_HEAD_DIM, (hd + 1) * HG_HEAD_DIM)
        scores = jnp.zeros((th, th), F32)
        for qw, kw, mask in levels:
            scores = scores + jnp.where(mask, _dot_nt(qw[:, sl], kw[:, sl]), 0.0)
        vh = vv[:, sl].astype(BF16)
        st = st_ref[hd]
        o = _dot(scores.astype(BF16), vh) + _dot_nt(q_in[:, sl], st.astype(BF16))
        st_ref[hd] = st * dec[:, sl] + _dot_tn(vh, k_out[:, sl])
        on = o * lax.rsqrt(jnp.mean(o * o, axis=-1, keepdims=True) + NORM_EPS) * on_ref[:, sl]
        gg = g_ref[0, :, sl]
        o_ref[0, :, sl] = (on * (gg * jax.nn.sigmoid(gg))).astype(o_ref.dtype)


def _hg_core(qfig, gamma, layer, o_norm, th):
    b, lp, d4 = qfig.shape
    d = d4 // 4
    depth = gamma.shape[0]
    spec = lambda n: pl.BlockSpec((1, th, d), lambda bi, t: (bi, t, n))
    return pl.pallas_call(
        functools.partial(_hg_kernel, layer=layer),
        grid=(b, lp // th),
        in_specs=[spec(0), spec(1), spec(2), spec(3), _const_spec((depth, d)), _const_spec((1, d))],
        out_specs=pl.BlockSpec((1, th, d), lambda bi, t: (bi, t, 0)),
        out_shape=jax.ShapeDtypeStruct((b, lp, d), BF16),
        scratch_shapes=[pltpu.VMEM((d // HG_HEAD_DIM, HG_HEAD_DIM, HG_HEAD_DIM), F32)],
        compiler_params=_cparams("parallel", "arbitrary"),
        name="hg_core",
    )(qfig, qfig, qfig, qfig, gamma.astype(F32), o_norm.astype(F32).reshape(1, d))


def _hg_layer(h, gain, w_in, gamma, layer, o_norm, w_out):
    b, lp, d = h.shape
    h2 = h.reshape(b * lp, d)
    qfig = _norm_proj(h2, gain, w_in.astype(BF16), F32).reshape(b, lp, 4 * d)
    y = _hg_core(qfig, gamma, layer, o_norm, LANES)
    return _out_proj(h2, y.reshape(b * lp, d), w_out.astype(BF16)).reshape(b, lp, d)


def kernel(x, meta_tokens, norm_mix, norm_ffn, s5_w_in, s5_lam_re, s5_lam_im, s5_log_dt, s5_b_re, s5_b_im, s5_c_re, s5_c_im, s5_d, s5_w_glu, rg_w_in, rg_conv_w, rg_conv_b, rg_w_a, rg_b_a, rg_w_i, rg_b_i, rg_lambda, rg_w_out, sb_w_qkv, sb_q_norm, sb_k_norm, sb_w_out, hg_w_in, hg_gamma, hg_o_norm, hg_w_out, ffn_w_up, ffn_conv_w, ffn_conv_b, ffn_w_down):
    bsz, seq, d = x.shape
    n_meta = meta_tokens.shape[0]
    length = n_meta + seq
    assert ffn_conv_w.shape[1] == FFN_CONV
    lp = -(-(length + FFN_CONV - 1) // SEQ_TILE) * SEQ_TILE
    meta =jnp.broadcast_to(meta_tokens.astype(x.dtype)[None], (bsz, n_meta, d))
    h = jnp.concatenate([meta, x, jnp.zeros((bsz, lp - length, d), x.dtype)], axis=1)
    depth = norm_mix.shape[0]
    n_mixers = 4
    for layer in range(depth):
        m, j = layer % n_mixers, layer // n_mixers
        g = norm_mix[layer].astype(F32)
        if m == 0:
            h = _s5_layer(h, g, s5_w_in[j], s5_lam_re[j], s5_lam_im[j], s5_log_dt[j], s5_b_re[j], s5_b_im[j],
                          s5_c_re[j], s5_c_im[j], s5_d[j], s5_w_glu[j])
        elif m == 1:
            h = _rg_layer(h, g, rg_w_in[j], rg_conv_w[j], rg_conv_b[j], rg_w_a[j], rg_b_a[j], rg_w_i[j], rg_b_i[j],
                          rg_lambda[j], rg_w_out[j])
        elif m == 2:
            h = _sb_layer(h, g, sb_w_qkv[j], sb_q_norm[j], sb_k_norm[j], sb_w_out[j])
        else:
            h = _hg_layer(h, g, hg_w_in[j], hg_gamma, layer, hg_o_norm[j], hg_w_out[j])
        h = _conv_ffn(h, norm_ffn[layer].astype(F32), ffn_w_up[layer].astype(BF16), ffn_conv_w[layer].astype(F32),
                      ffn_conv_b[layer].astype(F32), ffn_w_down[layer].astype(BF16))
    return h[:, n_meta:length]
```

```python
import functools
import math

import jax
import jax.numpy as jnp
from jax import lax
from jax.experimental import pallas as pl
from jax.experimental.pallas import tpu as pltpu

F32 = jnp.float32
BF16 = jnp.bfloat16

D_MODEL = 1024
N_META = 16
NORM_EPS = 1e-6
LANES = 128
SUBLANES = 8
SEQ_TILE = 256
ROW_TILE = 512
VMEM_LIMIT_BYTES = 56 * 1024 * 1024

FFN_CONV = 3
S5_GROUP = 16
S5_STATE = 64
RG_BLOCKS = 16
RG_C = 8.0
SB_HEAD_DIM = 64
SB_GROUP = 4
SB_MASK_BIAS = -1e9
HG_HEAD_DIM = 128
HG_HEADS = D_MODEL // HG_HEAD_DIM
HG_BLOCK = 16


def _cparams(*sem):
    return pltpu.CompilerParams(dimension_semantics=sem, vmem_limit_bytes=VMEM_LIMIT_BYTES)


def _const_spec(shape):
    nd = len(shape)
    return pl.BlockSpec(shape, lambda *_: (0,) * nd)


def _rms(x, g):
    return x * lax.rsqrt(jnp.mean(x * x, axis=-1, keepdims=True) + NORM_EPS) * g


def _sigmoid(x):
    return 0.5 * jnp.tanh(0.5 * x) + 0.5


def _silu(x):
    half = 0.5 * x
    return half * jnp.tanh(half) + half


def _dot(a, b):
    return jnp.dot(a, b, preferred_element_type=F32)


def _dot_nt(a, b):
    return lax.dot_general(a, b, (((1,), (1,)), ((), ())), preferred_element_type=F32)


def _dot_tn(a, b):
    return lax.dot_general(a, b, (((0,), (0,)), ((), ())), preferred_element_type=F32)


def _shift_rows(x, k, row, fill=0.0):
    n = x.shape[0]
    if k % SUBLANES == 0:
        return jnp.concatenate([jnp.full((k,) + x.shape[1:], fill, x.dtype), x[: n - k]], axis=0)
    return jnp.where(row >= k, pltpu.roll(x, k, 0), fill)


def _chunks(n, width):
    out, c = [], 0
    while c < n:
        w = min(width, n - c)
        out.append((c, w))
        c += w
    return out


def _proj_kernel(x_ref, g_ref, w_ref, o_ref):
    xn = _rms(x_ref[...], g_ref[...]).astype(BF16)
    for c, w in _chunks(o_ref.shape[-1], 512):
        o_ref[:, c:c + w] = _dot(xn, w_ref[:, c:c + w]).astype(o_ref.dtype)


def _norm_proj(h2, gain, w, out_dtype):
    m, d = h2.shape
    n = w.shape[1]
    tm = ROW_TILE if m % ROW_TILE == 0 else SEQ_TILE
    return pl.pallas_call(
        _proj_kernel,
        grid=(m // tm,),
        in_specs=[pl.BlockSpec((tm, d), lambda i: (i, 0)), _const_spec((1, d)), _const_spec((d, n))],
        out_specs=pl.BlockSpec((tm, n), lambda i: (i, 0)),
        out_shape=jax.ShapeDtypeStruct((m, n), out_dtype),
        compiler_params=_cparams("parallel"),
        name="norm_proj",
    )(h2, gain.reshape(1, d), w)


def _qkv_kernel(x_ref, g_ref, w_ref, avg_ref, qg_ref, kg_ref, o_ref):
    xn = _rms(x_ref[...], g_ref[...]).astype(BF16)
    d = x_ref.shape[-1]
    for c, w in _chunks(o_ref.shape[-1], 512):
        r = _dot(xn, w_ref[:, c:c + w])
        if c < 2 * d:
            ms = _dot((r * r).astype(BF16), avg_ref[...])
            gain = qg_ref[...] if c < d else kg_ref[...]
            r = r * lax.rsqrt(ms + NORM_EPS) * gain
        o_ref[:, c:c + w] = r.astype(o_ref.dtype)


def _qkv_proj(h2, gain, w, q_norm, k_norm):
    m, d = h2.shape
    n = w.shape[1]
    tm = ROW_TILE if m % ROW_TILE == 0 else SEQ_TILE
    heads_per_chunk = 512 // SB_HEAD_DIM
    blk = jnp.arange(512) // SB_HEAD_DIM
    avg = jnp.where(blk[:, None] == blk[None, :], 1.0 / SB_HEAD_DIM, 0.0).astype(BF16)
    scale = SB_HEAD_DIM ** -0.5
    qg = jnp.tile(q_norm.astype(F32) * scale, heads_per_chunk).reshape(1, 512)
    kg = jnp.tile(k_norm.astype(F32), heads_per_chunk).reshape(1, 512)
    return pl.pallas_call(
        _qkv_kernel,
        grid=(m // tm,),
        in_specs=[pl.BlockSpec((tm, d), lambda i: (i, 0)), _const_spec((1, d)), _const_spec((d, n)),
                  _const_spec((512, 512)), _const_spec((1, 512)), _const_spec((1, 512))],
        out_specs=pl.BlockSpec((tm, n), lambda i: (i, 0)),
        out_shape=jax.ShapeDtypeStruct((m, n), BF16),
        compiler_params=_cparams("parallel"),
        name="qkv_proj",
    )(h2, gain.reshape(1, d), w, avg, qg, kg)


def _out_kernel(h_ref, y_ref, w_ref, o_ref):
    y = y_ref[...]
    for c, w in _chunks(o_ref.shape[-1], 512):
        o_ref[:, c:c + w] = h_ref[:, c:c + w] + _dot(y, w_ref[:, c:c + w])


def _out_proj(h2, y2, w):
    m, d = h2.shape
    k = y2.shape[1]
    tm = ROW_TILE if m % ROW_TILE == 0 else SEQ_TILE
    return pl.pallas_call(
        _out_kernel,
        grid=(m // tm,),
        in_specs=[pl.BlockSpec((tm, d), lambda i: (i, 0)), pl.BlockSpec((tm, k), lambda i: (i, 0)),
                  _const_spec((k, d))],
        out_specs=pl.BlockSpec((tm, d), lambda i: (i, 0)),
        out_shape=jax.ShapeDtypeStruct((m, d), F32),
        compiler_params=_cparams("parallel"),
        name="out_proj",
    )(h2, y2, w)


def _glu_out_kernel(h_ref, y_ref, w_ref, o_ref):
    y = y_ref[...]
    d = o_ref.shape[-1]
    for c, w in _chunks(d, 512):
        a = _dot(y, w_ref[:, c:c + w])
        g = _dot(y, w_ref[:, d + c:d + c + w])
        o_ref[:, c:c + w] = h_ref[:, c:c + w] + a * _sigmoid(g)


def _glu_out_proj(h2, y2, w):
    m, d = h2.shape
    k = y2.shape[1]
    tm = ROW_TILE if m % ROW_TILE == 0 else SEQ_TILE
    return pl.pallas_call(
        _glu_out_kernel,
        grid=(m // tm,),
        in_specs=[pl.BlockSpec((tm, d), lambda i: (i, 0)), pl.BlockSpec((tm, k), lambda i: (i, 0)),
                  _const_spec((k, 2 * d))],
        out_specs=pl.BlockSpec((tm, d), lambda i: (i, 0)),
        out_shape=jax.ShapeDtypeStruct((m, d), F32),
        compiler_params=_cparams("parallel"),
        name="glu_out_proj",
    )(h2, y2, w)


def _ffn_kernel(h_ref, hp_ref, g_ref, wu_ref, cw_ref, cb_ref, wd_ref, o_ref, gated_ref, *, d_ff, halo, seq_len):
    i = pl.program_id(0)
    xc = h_ref[...]
    tm, d = xc.shape
    x = jnp.concatenate([hp_ref[...], xc], axis=0)
    row = lax.broadcasted_iota(jnp.int32, (halo + tm, 1), 0)
    pos = lax.rem(i * tm - halo + seq_len, seq_len) + row
    pos = jnp.where(pos >= seq_len, pos - seq_len, pos)
    xn = jnp.where(pos < seq_len - (FFN_CONV - 1), _rms(x, g_ref[...]), 0.0).astype(BF16)

    def conv(u, c, w):
        u1 = pltpu.roll(u, 1, 0)
        u2 = pltpu.roll(u, 2, 0)
        return (cb_ref[:, c:c + w] + cw_ref[2:3, c:c + w] * u[halo:] + cw_ref[1:2, c:c + w] * u1[halo:]
                + cw_ref[0:1, c:c + w] * u2[halo:])

    for c, w in _chunks(d_ff, 512):
        a = conv(_dot(xn, wu_ref[:, c:c + w]), c, w)
        b = conv(_dot(xn, wu_ref[:, d_ff + c:d_ff + c + w]), d_ff + c, w)
        gated_ref[:, c:c + w] = (_silu(a) * b).astype(BF16)
    gated = gated_ref[...]
    for c, w in _chunks(d, 512):
        o_ref[:, c:c + w] = xc[:, c:c + w] + _dot(gated, wd_ref[:, c:c + w])


def _conv_ffn(h, gain, w_up, conv_w, conv_b, w_down):
    b, lp, d = h.shape
    d_ff = w_down.shape[0]
    m = b * lp
    tm = ROW_TILE if m % ROW_TILE == 0 else SEQ_TILE
    halo = SUBLANES
    hb = tm // halo
    out = pl.pallas_call(
        functools.partial(_ffn_kernel, d_ff=d_ff, halo=halo, seq_len=lp),
        grid=(m // tm,),
        in_specs=[pl.BlockSpec((tm, d), lambda i: (i, 0)),
                  pl.BlockSpec((halo, d), lambda i: (jnp.maximum(i * hb - 1, 0), 0)),
                  _const_spec((1, d)), _const_spec((d, 2 * d_ff)), _const_spec((3, 2 * d_ff)),
                  _const_spec((1, 2 * d_ff)), _const_spec((d_ff, d))],
        out_specs=pl.BlockSpec((tm, d), lambda i: (i, 0)),
        out_shape=jax.ShapeDtypeStruct((m, d), F32),
        scratch_shapes=[pltpu.VMEM((tm, d_ff), BF16)],
        compiler_params=_cparams("parallel"),
        name="conv_ffn",
    )(h.reshape(m, d), h.reshape(m, d), gain.reshape(1, d), w_up, conv_w, conv_b.reshape(1, -1), w_down)
    return out.reshape(b, lp, d)


def _s5_kernel(u_ref, b_ref, c_ref, a_ref, al_ref, d_ref, o_ref, st_ref, blk_ref, *, nsteps):
    t = pl.program_id(2)

    @pl.when(t == 0)
    def _():
        st_ref[...] = jnp.zeros_like(st_ref)

    u = u_ref[0]
    ts = u.shape[0]
    nb = ts // SUBLANES
    inner = int(math.log2(SUBLANES))
    ub = u.astype(BF16)
    row = lax.broadcasted_iota(jnp.int32, u.shape, 0)
    row_in_block = row & (SUBLANES - 1)
    brow = lax.broadcasted_iota(jnp.int32, (nb, LANES), 0)
    keeps = [row_in_block >= (1 << s) for s in range(inner)]
    y = u * d_ref[...]

    def cmul(ar, ai, xr, xi):
        return ar * xr - ai * xi, ar * xi + ai * xr

    for q in range(4):
        xr = _dot(ub, b_ref[0, 2 * q])
        xi = _dot(ub, b_ref[0, 2 * q + 1])
        for s in range(inner):
            k = 1 << s
            keep = keeps[s]
            dr, di = cmul(a_ref[0, q, 0, s:s + 1, :], a_ref[0, q, 1, s:s + 1, :],
                          jnp.where(keep, pltpu.roll(xr, k, 0), 0.0), jnp.where(keep, pltpu.roll(xi, k, 0), 0.0))
            xr, xi = xr + dr, xi + di
        blk_ref[0] = xr
        blk_ref[1] = xi
        er = blk_ref[0, pl.ds(SUBLANES - 1, nb, stride=SUBLANES), :]
        ei = blk_ref[1, pl.ds(SUBLANES - 1, nb, stride=SUBLANES), :]
        x0r = st_ref[2 * q:2 * q + 1, :]
        x0i = st_ref[2 * q + 1:2 * q + 2, :]
        cr, ci = cmul(a_ref[0, q, 0, inner:inner + 1, :], a_ref[0, q, 1, inner:inner + 1, :], x0r, x0i)
        er = er + jnp.where(brow == 0, cr, 0.0)
        ei = ei + jnp.where(brow == 0, ci, 0.0)
        for s in range(nsteps - inner):
            k = 1 << s
            dr, di = cmul(a_ref[0, q, 0, inner + s:inner + s + 1, :], a_ref[0, q, 1, inner + s:inner + s + 1, :],
                          _shift_rows(er, k, brow), _shift_rows(ei, k, brow))
            er, ei = er + dr, ei + di
        st_ref[2 * q:2 * q + 1, :] = er[nb - 1:nb, :]
        st_ref[2 * q + 1:2 * q + 2, :] = ei[nb - 1:nb, :]
        pr = jnp.concatenate([jnp.broadcast_to(x0r, (SUBLANES, LANES))]
                             + [jnp.broadcast_to(er[k:k + 1, :], (SUBLANES, LANES)) for k in range(nb - 1)], axis=0)
        pi = jnp.concatenate([jnp.broadcast_to(x0i, (SUBLANES, LANES))]
                             + [jnp.broadcast_to(ei[k:k + 1, :], (SUBLANES, LANES)) for k in range(nb - 1)], axis=0)
        lr = jnp.concatenate([al_ref[0, q, 0]] * nb, axis=0)
        li = jnp.concatenate([al_ref[0, q, 1]] * nb, axis=0)
        dr, di = cmul(lr, li, pr, pi)
        xr, xi = xr + dr, xi + di
        y = y + _dot(xr.astype(BF16), c_ref[0, 2 * q]) + _dot(xi.astype(BF16), c_ref[0, 2 * q + 1])
    o_ref[0] = jax.nn.gelu(y).astype(o_ref.dtype)


def _s5_params(lam_re, lam_im, log_dt, b_re, b_im, c_re, c_im, nsteps):
    lr = lam_re.astype(F32)
    li = lam_im.astype(F32)
    dt = jnp.exp(log_dt.astype(F32))[:, None]
    mag = jnp.exp(lr * dt)
    ang = li * dt
    ab_re = mag * jnp.cos(ang)
    ab_im = mag * jnp.sin(ang)
    den = lr * lr + li * li
    cr = ((ab_re - 1.0) * lr + ab_im * li) / den
    ci = (ab_im * lr - (ab_re - 1.0) * li) / den
    br = b_re.astype(F32)
    bi = b_im.astype(F32)
    bb_re = cr[..., None] * br - ci[..., None] * bi
    bb_im = cr[..., None] * bi + ci[..., None] * br
    nblk = D_MODEL // LANES
    gpb = LANES // S5_GROUP
    npair = gpb // 2
    onehot = (jnp.arange(gpb)[None, None, :] == (2 * jnp.arange(npair)[:, None, None] + jnp.arange(2)[None, :, None]))
    onehot = onehot.astype(F32)

    def bsel(bb):
        bb = bb.reshape(nblk, npair, 2, S5_STATE, S5_GROUP)
        return jnp.einsum('kqmph,qmg->kqghmp', bb, onehot).reshape(nblk, npair, LANES, LANES)

    def csel(cc):
        cc = cc.astype(F32).reshape(nblk, npair, 2, S5_GROUP, S5_STATE)
        return jnp.einsum('kqmhp,qmg->kqmpgh', cc, onehot).reshape(nblk, npair, LANES, LANES)

    b_sel = jnp.stack([bsel(bb_re), bsel(bb_im)], axis=2).reshape(nblk, 2 * npair, LANES, LANES).astype(BF16)
    c_sel = jnp.stack([csel(c_re), -csel(c_im)], axis=2).reshape(nblk, 2 * npair, LANES, LANES).astype(BF16)
    ar = ab_re.reshape(-1, LANES)
    ai = ab_im.reshape(-1, LANES)
    pr, pi = [], []
    for _ in range(nsteps):
        pr.append(ar)
        pi.append(ai)
        ar, ai = ar * ar - ai * ai, 2.0 * ar * ai
    apow = jnp.stack([jnp.stack(pr, axis=1), jnp.stack(pi, axis=1)], axis=1)
    ar = ab_re.reshape(-1, LANES)
    ai = ab_im.reshape(-1, LANES)
    lr, li = [ar], [ai]
    for _ in range(SUBLANES - 1):
        lr, li = lr + [lr[-1] * ar - li[-1] * ai], li + [lr[-1] * ai + li[-1] * ar]
    alin = jnp.stack([jnp.stack(lr, axis=1), jnp.stack(li, axis=1)], axis=1)
    return (b_sel, c_sel, apow.reshape(nblk, npair, 2, nsteps, LANES),
            alin.reshape(nblk, npair, 2, SUBLANES, LANES))


def _s5_core(u, b_sel, c_sel, apow, alin, d_skip, ts):
    b, lp, d = u.shape
    nsteps = apow.shape[3]
    nblk = d // LANES
    return pl.pallas_call(
        functools.partial(_s5_kernel, nsteps=nsteps),
        grid=(b, nblk, lp // ts),
        in_specs=[pl.BlockSpec((1, ts, LANES), lambda bi, k, t: (bi, t, k)),
                  pl.BlockSpec((1, 8, LANES, LANES), lambda bi, k, t: (k, 0, 0, 0)),
                  pl.BlockSpec((1, 8, LANES, LANES), lambda bi, k, t: (k, 0, 0, 0)),
                  pl.BlockSpec((1, 4, 2, nsteps, LANES), lambda bi, k, t: (k, 0, 0, 0, 0)),
                  pl.BlockSpec((1, 4, 2, SUBLANES, LANES), lambda bi, k, t: (k, 0, 0, 0, 0)),
                  pl.BlockSpec((1, LANES), lambda bi, k, t: (0, k))],
        out_specs=pl.BlockSpec((1, ts, LANES), lambda bi, k, t: (bi, t, k)),
        out_shape=jax.ShapeDtypeStruct((b, lp, d), BF16),
        scratch_shapes=[pltpu.VMEM((8, LANES), F32), pltpu.VMEM((2, ts, LANES), F32)],
        compiler_params=_cparams("parallel", "parallel", "arbitrary"),
        name="s5_core",
    )(u, b_sel, c_sel, apow, alin, d_skip.reshape(1, d))


def _s5_layer(h, gain, w_in, lam_re, lam_im, log_dt, b_re, b_im, c_re, c_im, d_skip, w_glu):
    b, lp, d = h.shape
    ts = SEQ_TILE
    b_sel, c_sel, apow, alin = _s5_params(lam_re, lam_im, log_dt, b_re, b_im, c_re, c_im, int(math.log2(ts)))
    h2 = h.reshape(b * lp, d)
    u = _norm_proj(h2, gain, w_in.astype(BF16), F32).reshape(b, lp, d)
    y = _s5_core(u, b_sel, c_sel, apow, alin, d_skip.astype(F32), ts)
    return _glu_out_proj(h2, y.reshape(b * lp, d), w_glu.astype(BF16)).reshape(b, lp, d)


def _rg_kernel(gate_ref, xb_ref, xbp_ref, cw_ref, cb_ref, wa_ref, ba_ref, wi_ref, bi_ref, lam_ref, o_ref,
               xc_ref, xcb_ref, st_ref, blk_ref, *, nsteps, halo):
    i = pl.program_id(1)

    @pl.when(i == 0)
    def _():
        st_ref[...] = jnp.zeros_like(st_ref)

    tr, width = xc_ref.shape
    keep = (i > 0).astype(F32)
    for c, w in _chunks(width, LANES):
        xx = jnp.concatenate([xbp_ref[0, :, c:c + w] * keep, xb_ref[0, :, c:c + w]], axis=0)
        conv = cb_ref[:, c:c + w] + cw_ref[3:4, c:c + w] * xx[halo:]
        for tap in range(1, 4):
            conv = conv + cw_ref[3 - tap:4 - tap, c:c + w] * pltpu.roll(xx, tap, 0)[halo:]
        xc_ref[:, c:c + w] = conv
        xcb_ref[:, c:c + w] = conv.astype(BF16)

    nb = tr // SUBLANES
    inner = int(math.log2(SUBLANES))
    row_in_block = lax.broadcasted_iota(jnp.int32, (tr, 2 * LANES), 0) & (SUBLANES - 1)
    keeps = [row_in_block >= (1 << s) for s in range(inner)]
    brow = lax.broadcasted_iota(jnp.int32, (nb, 2 * LANES), 0)
    for c, w in _chunks(width, 2 * LANES):
        xcb = xcb_ref[...]
        r = _sigmoid(_dot(xcb, wa_ref[:, c:c + w]) + ba_ref[:, c:c + w])
        ig = _sigmoid(_dot(xcb, wi_ref[:, c:c + w]) + bi_ref[:, c:c + w])
        lam = lam_ref[:, c:c + w]
        softplus_neg_lam = jnp.maximum(-lam, 0.0) + jnp.log1p(jnp.exp(-jnp.abs(lam)))
        log_a = (-RG_C * r) * softplus_neg_lam
        a = jnp.exp(log_a)
        bv = jnp.exp(0.5 * jnp.log(-jnp.tanh(log_a) * (a * a + 1.0))) * (ig * xc_ref[:, c:c + w])
        for s in range(inner):
            k = 1 << s
            keep = keeps[s][:, :w]
            bv = bv + a * jnp.where(keep, pltpu.roll(bv, k, 0), 0.0)
            a = a * jnp.where(keep, pltpu.roll(a, k, 0), 1.0)
        ends = []
        for m, val in enumerate((a, bv)):
            for hf in range(w // LANES):
                blk_ref[2 * m + hf] = val[:, hf * LANES:(hf + 1) * LANES]
            ends.append(jnp.concatenate([blk_ref[2 * m + hf, pl.ds(SUBLANES - 1, nb, stride=SUBLANES), :]
                                         for hf in range(w // LANES)], axis=1))
        ea, eb = ends
        h0 = st_ref[:, c:c + w]
        bw = brow[:, :w]
        eb = eb + jnp.where(bw == 0, ea * h0, 0.0)
        for s in range(nsteps - inner):
            k = 1 << s
            eb = eb + ea * _shift_rows(eb, k, bw)
            ea = ea * _shift_rows(ea, k, bw, fill=1.0)
        st_ref[:, c:c + w] = eb[nb - 1:nb, :]
        enter = jnp.concatenate([jnp.broadcast_to(h0, (SUBLANES, w))]
                                + [jnp.broadcast_to(eb[k:k + 1, :], (SUBLANES, w)) for k in range(nb - 1)], axis=0)
        hh = bv + a * enter
        o_ref[0, :, c:c + w] = (hh * jax.nn.gelu(gate_ref[0, :, c:c + w])).astype(o_ref.dtype)


def _rg_core(gx, conv_w, conv_b, w_a, b_a, w_i, b_i, lam, tr):
    b, lp, w2 = gx.shape
    width = w2 // 2
    nl = width // LANES
    halo = SUBLANES
    hb = tr // halo
    vec = lambda v: v.reshape(1, width)
    return pl.pallas_call(
        functools.partial(_rg_kernel, nsteps=int(math.log2(tr)), halo=halo),
        grid=(b, lp // tr),
        in_specs=[pl.BlockSpec((1, tr, width), lambda bi, i: (bi, i, 0)),
                  pl.BlockSpec((1, tr, width), lambda bi, i: (bi, i, 1)),
                  pl.BlockSpec((1, halo, width), lambda bi, i: (bi, jnp.maximum(i * hb - 1, 0), 1)),
                  _const_spec((4, width)), _const_spec((1, width)),
                  _const_spec((width, width)), _const_spec((1, width)),
                  _const_spec((width, width)), _const_spec((1, width)), _const_spec((1, width))],
        out_specs=pl.BlockSpec((1, tr, width), lambda bi, i: (bi, i, 0)),
        out_shape=jax.ShapeDtypeStruct((b, lp, width), BF16),
        scratch_shapes=[pltpu.VMEM((tr, width), F32), pltpu.VMEM((tr, width), BF16), pltpu.VMEM((1, width), F32),
                        pltpu.VMEM((4, tr, LANES), F32)],
        compiler_params=_cparams("parallel", "arbitrary"),
        name="rg_core",
    )(gx, gx, gx, conv_w, vec(conv_b), w_a, vec(b_a), w_i, vec(b_i), vec(lam))


def _block_diag(w):
    n, j, k = w.shape
    return jnp.einsum('njk,nm->njmk', w, jnp.eye(n, dtype=w.dtype)).reshape(n * j, n * k)


def _rg_layer(h, gain, w_in, conv_w, conv_b, w_a, b_a, w_i, b_i, lam, w_out):
    b, lp, d = h.shape
    width = w_out.shape[0]
    wp = -(-width // LANES) * LANES
    pad = wp - width
    padc = lambda v: jnp.pad(v.astype(F32), [(0, 0)] * (v.ndim - 1) + [(0, pad)])
    w_in_p = jnp.concatenate([padc(w_in[:, :width]), padc(w_in[:, width:])], axis=1).astype(BF16)
    dense = lambda w: jnp.pad(_block_diag(w.astype(F32)), ((0, pad), (0, pad))).astype(BF16)
    h2 = h.reshape(b * lp, d)
    gx = _norm_proj(h2, gain, w_in_p, F32).reshape(b, lp, 2 * wp)
    y = _rg_core(gx, padc(conv_w), padc(conv_b), dense(w_a), padc(b_a), dense(w_i), padc(b_i), padc(lam), SEQ_TILE)
    w_out_p = jnp.pad(w_out.astype(F32), ((0, pad), (0, 0))).astype(BF16)
    return _out_proj(h2, y.reshape(b * lp, wp), w_out_p).reshape(b, lp, d)


def _sb_kernel(q_ref, k_ref, v_ref, tri_ref, bias_ref, o_ref, acc_ref, later_ref):
    i = pl.program_id(2)
    q = q_ref[0]
    tq = q.shape[0]
    lane = lax.broadcasted_iota(jnp.int32, (1, LANES), 1)
    head0 = lane < SB_HEAD_DIM
    zero = jnp.zeros_like(q)
    q2 = jnp.concatenate([jnp.where(head0, q, zero), jnp.where(head0, zero, q)], axis=0)
    tri = tri_ref[...]
    acc_ref[...] = jnp.zeros_like(acc_ref)
    later_ref[...] = jnp.zeros_like(later_ref)

    def tile_start(j):
        return pl.multiple_of(jnp.maximum(j, 0) * tq, tq)

    def scores(j):
        return _dot_nt(q2, k_ref[0, pl.ds(tile_start(j), tq), :])

    def weights(z, later):
        neg_abs = lax.bitcast_convert_type(lax.bitcast_convert_type(z, jnp.uint32) | jnp.uint32(0x80000000), F32)
        log_beta = jnp.minimum(z, 0.0) - jnp.log(1.0 + jnp.exp(neg_abs))
        log_keep = log_beta - z
        between = _dot(log_keep.astype(BF16), tri)
        w = jnp.exp(log_beta + between + later)
        return w.astype(BF16), jnp.sum(log_keep, axis=-1, keepdims=True)

    def weighted_values(j, w):
        return _dot(w, v_ref[0, pl.ds(tile_start(j), tq), :])

    def group(j, n, first_is_diagonal):
        later = later_ref[...]
        pv = None
        for t in range(n):
            z = scores(j - t)
            if t == 0:
                z = z + bias_ref[jnp.where(first_is_diagonal, 0, 1)]
            w, rs = weights(z, later)
            part = weighted_values(j - t, w)
            pv = part if pv is None else pv + part
            later = later + rs
        acc_ref[...] += pv
        later_ref[...] = later

    n_tiles = i + 1
    n_full = n_tiles // SB_GROUP

    def body(g, carry):
        group(i - SB_GROUP * g, SB_GROUP, g == 0)
        return carry

    lax.fori_loop(0, n_full, body, 0)
    rem = n_tiles - SB_GROUP * n_full
    size = SB_GROUP // 2
    while size >= 1:
        @pl.when((rem & size) != 0)
        def _(size=size):
            first = (rem & (2 * size - 1)) - 1
            group(first, size, first == i)
        size //= 2

    acc = acc_ref[...]
    o_ref[0] = jnp.where(head0, acc[:tq], acc[tq:]).astype(o_ref.dtype)


def _sb_core(qkv, tq):
    b, lp, d3 = qkv.shape
    d = d3 // 3
    npair = d // LANES
    idx = jnp.arange(tq)
    tri = (idx[:, None] > idx[None, :]).astype(BF16)
    diag = jnp.where(idx[None, :] < idx[:, None], 0.0, SB_MASK_BIAS).astype(F32)
    bias = jnp.stack([jnp.concatenate([diag, diag], axis=0), jnp.zeros((2 * tq, tq), F32)])
    return pl.pallas_call(
        _sb_kernel,
        grid=(b, npair, lp // tq),
        in_specs=[pl.BlockSpec((1, tq, LANES), lambda bi, p, i: (bi, i, p)),
                  pl.BlockSpec((1, lp, LANES), lambda bi, p, i: (bi, 0, npair + p)),
                  pl.BlockSpec((1, lp, LANES), lambda bi, p, i: (bi, 0, 2 * npair + p)),
                  _const_spec((tq, tq)), _const_spec((2, 2 * tq, tq))],
        out_specs=pl.BlockSpec((1, tq, LANES), lambda bi, p, i: (bi, i, p)),
        out_shape=jax.ShapeDtypeStruct((b, lp, d), BF16),
        scratch_shapes=[pltpu.VMEM((2 * tq, LANES), F32), pltpu.VMEM((2 * tq, 1), F32)],
        compiler_params=_cparams("parallel", "parallel", "arbitrary"),
        name="sb_core",
    )(qkv, qkv, qkv, tri, bias)


def _sb_layer(h, gain, w_qkv, q_norm, k_norm, w_out):
    b, lp, d = h.shape
    h2 = h.reshape(b * lp, d)
    qkv = _qkv_proj(h2, gain, w_qkv.astype(BF16), q_norm, k_norm).reshape(b, lp, 3 * d)
    o = _sb_core(qkv, SEQ_TILE)
    return _out_proj(h2, o.reshape(b * lp, d), w_out.astype(BF16)).reshape(b, lp, d)


def _hg_kernel(q_ref, f_ref, v_ref, g_ref, gam_ref, on_ref, o_ref, st_ref, *, layer):
    t = pl.program_id(1)

    @pl.when(t == 0)
    def _():
        st_ref[...] = jnp.zeros_like(st_ref)

    gam = gam_ref[...]
    e = jnp.exp(gam - jnp.max(gam, axis=0, keepdims=True))
    lb = jnp.sum(e[:layer], axis=0, keepdims=True) / jnp.sum(e, axis=0, keepdims=True)
    fz = f_ref[0]
    th, d = fz.shape
    sig = _sigmoid(fz)
    logf = jnp.log(lb + (1.0 - lb) * sig)
    kk = (1.0 - lb) * (1.0 - sig)
    row = lax.broadcasted_iota(jnp.int32, (th, d), 0)
    cum = logf
    for s in range(int(math.log2(th))):
        cum = cum + _shift_rows(cum, 1 << s, row)
    total = cum[th - 1:th, :]
    qv = q_ref[0]
    vv = v_ref[0]
    rowc = lax.broadcasted_iota(jnp.int32, (th, th), 0)
    colc = lax.broadcasted_iota(jnp.int32, (th, th), 1)

    levels = []
    w = th // 2
    while w >= HG_BLOCK:
        blk = 2 * w
        ref = jnp.concatenate([jnp.broadcast_to(cum[s0 + w - 1:s0 + w, :], (blk, d)) for s0 in range(0, th, blk)],
                              axis=0)
        qw = (qv * jnp.exp(jnp.minimum(cum - ref, 0.0))).astype(BF16)
        kw = (kk * jnp.exp(jnp.minimum(ref - cum, 0.0))).astype(BF16)
        sh = int(math.log2(blk))
        mask = jnp.logical_and((rowc >> sh) == (colc >> sh),
                               jnp.logical_and((rowc & (blk - 1)) >= w, (colc & (blk - 1)) < w))
        levels.append((qw, kw, mask))
        w //= 2
    ref = jnp.concatenate([jnp.zeros((HG_BLOCK, d), F32)]
                          + [jnp.broadcast_to(cum[s0 - 1:s0, :], (HG_BLOCK, d)) for s0 in range(HG_BLOCK, th, HG_BLOCK)],
                          axis=0)
    qd = (qv * jnp.exp(cum - ref)).astype(BF16)
    kd = (kk * jnp.exp(ref - cum)).astype(BF16)
    sh = int(math.log2(HG_BLOCK))
    levels.append((qd, kd, jnp.logical_and((rowc >> sh) == (colc >> sh), colc <= rowc)))
    q_in = (qv * jnp.exp(cum)).astype(BF16)
    k_out = (kk * jnp.exp(total - cum)).astype(BF16)
    dec = jnp.exp(total)

    for hd in range(d // HG_HEAD_DIM):
        sl = slice(hd * HG_HEAD_DIM, (hd + 1) * HG_HEAD_DIM)
        scores = jnp.zeros((th, th), F32)
        for qw, kw, mask in levels:
            scores = scores + jnp.where(mask, _dot_nt(qw[:, sl], kw[:, sl]), 0.0)
        vh = vv[:, sl].astype(BF16)
        st = st_ref[hd]
        o = _dot(scores.astype(BF16), vh) + _dot_nt(q_in[:, sl], st.astype(BF16))
        st_ref[hd] = st * dec[:, sl] + _dot_tn(vh, k_out[:, sl])
        on = o * lax.rsqrt(jnp.mean(o * o, axis=-1, keepdims=True) + NORM_EPS) * on_ref[:, sl]
        gg = g_ref[0, :, sl]
        o_ref[0, :, sl] = (on * _silu(gg)).astype(o_ref.dtype)


def _hg_core(qfig, gamma, layer, o_norm, th):
    b, lp, d4 = qfig.shape
    d = d4 // 4
    depth = gamma.shape[0]
    spec = lambda n: pl.BlockSpec((1, th, d), lambda bi, t: (bi, t, n))
    return pl.pallas_call(
        functools.partial(_hg_kernel, layer=layer),
        grid=(b, lp // th),
        in_specs=[spec(0), spec(1), spec(2), spec(3), _const_spec((depth, d)), _const_spec((1, d))],
        out_specs=pl.BlockSpec((1, th, d), lambda bi, t: (bi, t, 0)),
        out_shape=jax.ShapeDtypeStruct((b, lp, d), BF16),
        scratch_shapes=[pltpu.VMEM((d // HG_HEAD_DIM, HG_HEAD_DIM, HG_HEAD_DIM), F32)],
        compiler_params=_cparams("parallel", "arbitrary"),
        name="hg_core",
    )(qfig, qfig, qfig, qfig, gamma.astype(F32), o_norm.astype(F32).reshape(1, d))


def _hg_layer(h, gain, w_in, gamma, layer, o_norm, w_out):
    b, lp, d = h.shape
    h2 = h.reshape(b * lp, d)
    qfig = _norm_proj(h2, gain, w_in.astype(BF16), F32).reshape(b, lp, 4 * d)
    y = _hg_core(qfig, gamma, layer, o_norm, LANES)
    return _out_proj(h2, y.reshape(b * lp, d), w_out.astype(BF16)).reshape(b, lp, d)


def kernel(x, meta_tokens, norm_mix, norm_ffn, s5_w_in, s5_lam_re, s5_lam_im, s5_log_dt, s5_b_re, s5_b_im, s5_c_re, s5_c_im, s5_d, s5_w_glu, rg_w_in, rg_conv_w, rg_conv_b, rg_w_a, rg_b_a, rg_w_i, rg_b_i, rg_lambda, rg_w_out, sb_w_qkv, sb_q_norm, sb_k_norm, sb_w_out, hg_w_in, hg_gamma, hg_o_norm, hg_w_out, ffn_w_up, ffn_conv_w, ffn_conv_b, ffn_w_down):
    bsz, seq, d = x.shape
    n_meta = meta_tokens.shape[0]
    length = n_meta + seq
    assert ffn_conv_w.shape[1] == FFN_CONV
    lp = -(-(length + FFN_CONV - 1) // SEQ_TILE) * SEQ_TILE
    meta =jnp.broadcast_to(meta_tokens.astype(x.dtype)[None], (bsz, n_meta, d))
    h = jnp.concatenate([meta, x, jnp.zeros((bsz, lp - length, d), x.dtype)], axis=1)
    depth = norm_mix.shape[0]
    n_mixers = 4
    for layer in range(depth):
        m, j = layer % n_mixers, layer // n_mixers
        g = norm_mix[layer].astype(F32)
        if m == 0:
            h = _s5_layer(h, g, s5_w_in[j], s5_lam_re[j], s5_lam_im[j], s5_log_dt[j], s5_b_re[j], s5_b_im[j],
                          s5_c_re[j], s5_c_im[j], s5_d[j], s5_w_glu[j])
        elif m == 1:
            h = _rg_layer(h, g, rg_w_in[j], rg_conv_w[j], rg_conv_b[j], rg_w_a[j], rg_b_a[j], rg_w_i[j], rg_b_i[j],
                          rg_lambda[j], rg_w_out[j])
        elif m == 2:
            h = _sb_layer(h, g, sb_w_qkv[j], sb_q_norm[j], sb_k_norm[j], sb_w_out[j])
        else:
            h = _hg_layer(h, g, hg_w_in[j], hg_gamma, layer, hg_o_norm[j], hg_w_out[j])
        h = _conv_ffn(h, norm_ffn[layer].astype(F32), ffn_w_up[layer].astype(BF16), ffn_conv_w[layer].astype(F32),
                      ffn_conv_b[layer].astype(F32), ffn_w_down[layer].astype(BF16))
    return h[:, n_meta:length]
```

```python
import functools
import math

import jax
import jax.numpy as jnp
from jax import lax
from jax.experimental import pallas as pl
from jax.experimental.pallas import tpu as pltpu

F32 = jnp.float32
BF16 = jnp.bfloat16

D_MODEL = 1024
N_META = 16
NORM_EPS = 1e-6
LANES = 128
SUBLANES = 8
BF16_ROWS = 16
SEQ_TILE = 256
ROW_TILE = 512
VMEM_LIMIT_BYTES = 56 * 1024 * 1024

FFN_CONV = 3
S5_GROUP = 16
S5_STATE = 64
RG_BLOCKS = 16
RG_C = 8.0
SB_HEAD_DIM = 64
SB_GROUP = 4
SB_MASK_BIAS = -1e9
HG_HEAD_DIM = 128
HG_HEADS = D_MODEL // HG_HEAD_DIM
HG_BLOCK = 16


def _cparams(*sem):
    return pltpu.CompilerParams(dimension_semantics=sem, vmem_limit_bytes=VMEM_LIMIT_BYTES)


def _const_spec(shape):
    nd = len(shape)
    return pl.BlockSpec(shape, lambda *_: (0,) * nd)


def _rms(x, g):
    return x * lax.rsqrt(jnp.mean(x * x, axis=-1, keepdims=True) + NORM_EPS) * g


def _sigmoid(x):
    return 0.5 * jnp.tanh(0.5 * x) + 0.5


def _silu(x):
    half = 0.5 * x
    return half * jnp.tanh(half) + half


def _dot(a, b):
    return jnp.dot(a, b, preferred_element_type=F32)


def _dot_nt(a, b):
    return lax.dot_general(a, b, (((1,), (1,)), ((), ())), preferred_element_type=F32)


def _dot_tn(a, b):
    return lax.dot_general(a, b, (((0,), (0,)), ((), ())), preferred_element_type=F32)


def _shift_rows(x, k, row, fill=0.0):
    n = x.shape[0]
    if k % SUBLANES == 0:
        return jnp.concatenate([jnp.full((k,) + x.shape[1:], fill, x.dtype), x[: n - k]], axis=0)
    return jnp.where(row >= k, pltpu.roll(x, k, 0), fill)


def _chunks(n, width):
    out, c = [], 0
    while c < n:
        w = min(width, n - c)
        out.append((c, w))
        c += w
    return out


def _proj_kernel(x_ref, g_ref, w_ref, o_ref):
    xn = _rms(x_ref[...], g_ref[...]).astype(BF16)
    for c, w in _chunks(o_ref.shape[-1], 512):
        o_ref[:, c:c + w] = _dot(xn, w_ref[:, c:c + w]).astype(o_ref.dtype)


def _norm_proj(h2, gain, w, out_dtype):
    m, d = h2.shape
    n = w.shape[1]
    tm = ROW_TILE if m % ROW_TILE == 0 else SEQ_TILE
    return pl.pallas_call(
        _proj_kernel,
        grid=(m // tm,),
        in_specs=[pl.BlockSpec((tm, d), lambda i: (i, 0)), _const_spec((1, d)), _const_spec((d, n))],
        out_specs=pl.BlockSpec((tm, n), lambda i: (i, 0)),
        out_shape=jax.ShapeDtypeStruct((m, n), out_dtype),
        compiler_params=_cparams("parallel"),
        name="norm_proj",
    )(h2, gain.reshape(1, d), w)


def _qkv_kernel(x_ref, g_ref, w_ref, avg_ref, qg_ref, kg_ref, o_ref):
    xn = _rms(x_ref[...], g_ref[...]).astype(BF16)
    d = x_ref.shape[-1]
    for c, w in _chunks(o_ref.shape[-1], 512):
        r = _dot(xn, w_ref[:, c:c + w])
        if c < 2 * d:
            ms = _dot((r * r).astype(BF16), avg_ref[...])
            gain = qg_ref[...] if c < d else kg_ref[...]
            r = r * lax.rsqrt(ms + NORM_EPS) * gain
        o_ref[:, c:c + w] = r.astype(o_ref.dtype)


def _qkv_proj(h2, gain, w, q_norm, k_norm):
    m, d = h2.shape
    n = w.shape[1]
    tm = ROW_TILE if m % ROW_TILE == 0 else SEQ_TILE
    heads_per_chunk = 512 // SB_HEAD_DIM
    blk = jnp.arange(512) // SB_HEAD_DIM
    avg = jnp.where(blk[:, None] == blk[None, :], 1.0 / SB_HEAD_DIM, 0.0).astype(BF16)
    scale = SB_HEAD_DIM ** -0.5
    qg = jnp.tile(q_norm.astype(F32) * scale, heads_per_chunk).reshape(1, 512)
    kg = jnp.tile(k_norm.astype(F32), heads_per_chunk).reshape(1, 512)
    return pl.pallas_call(
        _qkv_kernel,
        grid=(m // tm,),
        in_specs=[pl.BlockSpec((tm, d), lambda i: (i, 0)), _const_spec((1, d)), _const_spec((d, n)),
                  _const_spec((512, 512)), _const_spec((1, 512)), _const_spec((1, 512))],
        out_specs=pl.BlockSpec((tm, n), lambda i: (i, 0)),
        out_shape=jax.ShapeDtypeStruct((m, n), BF16),
        compiler_params=_cparams("parallel"),
        name="qkv_proj",
    )(h2, gain.reshape(1, d), w, avg, qg, kg)


def _out_kernel(h_ref, y_ref, w_ref, o_ref):
    y = y_ref[...]
    for c, w in _chunks(o_ref.shape[-1], 512):
        o_ref[:, c:c + w] = h_ref[:, c:c + w] + _dot(y, w_ref[:, c:c + w])


def _out_proj(h2, y2, w):
    m, d = h2.shape
    k = y2.shape[1]
    tm = ROW_TILE if m % ROW_TILE == 0 else SEQ_TILE
    return pl.pallas_call(
        _out_kernel,
        grid=(m // tm,),
        in_specs=[pl.BlockSpec((tm, d), lambda i: (i, 0)), pl.BlockSpec((tm, k), lambda i: (i, 0)),
                  _const_spec((k, d))],
        out_specs=pl.BlockSpec((tm, d), lambda i: (i, 0)),
        out_shape=jax.ShapeDtypeStruct((m, d), F32),
        compiler_params=_cparams("parallel"),
        name="out_proj",
    )(h2, y2, w)


def _glu_out_kernel(h_ref, y_ref, w_ref, o_ref):
    y = y_ref[...]
    d = o_ref.shape[-1]
    for c, w in _chunks(d, 512):
        a = _dot(y, w_ref[:, c:c + w])
        g = _dot(y, w_ref[:, d + c:d + c + w])
        o_ref[:, c:c + w] = h_ref[:, c:c + w] + a * _sigmoid(g)


def _glu_out_proj(h2, y2, w):
    m, d = h2.shape
    k = y2.shape[1]
    tm = ROW_TILE if m % ROW_TILE == 0 else SEQ_TILE
    return pl.pallas_call(
        _glu_out_kernel,
        grid=(m // tm,),
        in_specs=[pl.BlockSpec((tm, d), lambda i: (i, 0)), pl.BlockSpec((tm, k), lambda i: (i, 0)),
                  _const_spec((k, 2 * d))],
        out_specs=pl.BlockSpec((tm, d), lambda i: (i, 0)),
        out_shape=jax.ShapeDtypeStruct((m, d), F32),
        compiler_params=_cparams("parallel"),
        name="glu_out_proj",
    )(h2, y2, w)


def _ffn_kernel(h_ref, hp_ref, g_ref, wu_ref, cw_ref, cb_ref, wd_ref, o_ref, gated_ref, *, d_ff, halo, seq_len):
    i = pl.program_id(0)
    xc = h_ref[...]
    tm, d = xc.shape
    x = jnp.concatenate([hp_ref[...], xc], axis=0)
    row = lax.broadcasted_iota(jnp.int32, (halo + tm, 1), 0)
    pos = lax.rem(i * tm - halo + seq_len, seq_len) + row
    pos = jnp.where(pos >= seq_len, pos - seq_len, pos)
    xn = jnp.where(pos < seq_len - (FFN_CONV - 1), _rms(x, g_ref[...]), 0.0).astype(BF16)

    def conv(u, c, w):
        u1 = pltpu.roll(u, 1, 0)
        u2 = pltpu.roll(u, 2, 0)
        return (cb_ref[:, c:c + w] + cw_ref[2:3, c:c + w] * u[halo:] + cw_ref[1:2, c:c + w] * u1[halo:]
                + cw_ref[0:1, c:c + w] * u2[halo:])

    for c, w in _chunks(d_ff, 512):
        a = conv(_dot(xn, wu_ref[:, c:c + w]), c, w)
        b = conv(_dot(xn, wu_ref[:, d_ff + c:d_ff + c + w]), d_ff + c, w)
        gated_ref[:, c:c + w] = (_silu(a) * b).astype(BF16)
    gated = gated_ref[...]
    for c, w in _chunks(d, 512):
        o_ref[:, c:c + w] = xc[:, c:c + w] + _dot(gated, wd_ref[:, c:c + w])


def _conv_ffn(h, gain, w_up, conv_w, conv_b, w_down):
    b, lp, d = h.shape
    d_ff = w_down.shape[0]
    m = b * lp
    tm = ROW_TILE if m % ROW_TILE == 0 else SEQ_TILE
    halo = SUBLANES
    hb = tm // halo
    out = pl.pallas_call(
        functools.partial(_ffn_kernel, d_ff=d_ff, halo=halo, seq_len=lp),
        grid=(m // tm,),
        in_specs=[pl.BlockSpec((tm, d), lambda i: (i, 0)),
                  pl.BlockSpec((halo, d), lambda i: (jnp.maximum(i * hb - 1, 0), 0)),
                  _const_spec((1, d)), _const_spec((d, 2 * d_ff)), _const_spec((3, 2 * d_ff)),
                  _const_spec((1, 2 * d_ff)), _const_spec((d_ff, d))],
        out_specs=pl.BlockSpec((tm, d), lambda i: (i, 0)),
        out_shape=jax.ShapeDtypeStruct((m, d), F32),
        scratch_shapes=[pltpu.VMEM((tm, d_ff), BF16)],
        compiler_params=_cparams("parallel"),
        name="conv_ffn",
    )(h.reshape(m, d), h.reshape(m, d), gain.reshape(1, d), w_up, conv_w, conv_b.reshape(1, -1), w_down)
    return out.reshape(b, lp, d)


def _s5_kernel(u_ref, b_ref, c_ref, a_ref, al_ref, d_ref, o_ref, st_ref, blk_ref, *, nsteps):
    t = pl.program_id(2)

    @pl.when(t == 0)
    def _():
        st_ref[...] = jnp.zeros_like(st_ref)

    u = u_ref[0]
    ts = u.shape[0]
    nb = ts // SUBLANES
    inner = int(math.log2(SUBLANES))
    ub = u.astype(BF16)
    row = lax.broadcasted_iota(jnp.int32, u.shape, 0)
    row_in_block = row & (SUBLANES - 1)
    brow = lax.broadcasted_iota(jnp.int32, (nb, LANES), 0)
    keeps = [row_in_block >= (1 << s) for s in range(inner)]
    y = u * d_ref[...]

    def cmul(ar, ai, xr, xi):
        return ar * xr - ai * xi, ar * xi + ai * xr

    for q in range(4):
        xr = _dot(ub, b_ref[0, 2 * q])
        xi = _dot(ub, b_ref[0, 2 * q + 1])
        for s in range(inner):
            k = 1 << s
            keep = keeps[s]
            dr, di = cmul(a_ref[0, q, 0, s:s + 1, :], a_ref[0, q, 1, s:s + 1, :],
                          jnp.where(keep, pltpu.roll(xr, k, 0), 0.0), jnp.where(keep, pltpu.roll(xi, k, 0), 0.0))
            xr, xi = xr + dr, xi + di
        blk_ref[0] = xr
        blk_ref[1] = xi
        er = blk_ref[0, pl.ds(SUBLANES - 1, nb, stride=SUBLANES), :]
        ei = blk_ref[1, pl.ds(SUBLANES - 1, nb, stride=SUBLANES), :]
        x0r = st_ref[2 * q:2 * q + 1, :]
        x0i = st_ref[2 * q + 1:2 * q + 2, :]
        cr, ci = cmul(a_ref[0, q, 0, inner:inner + 1, :], a_ref[0, q, 1, inner:inner + 1, :], x0r, x0i)
        er = er + jnp.where(brow == 0, cr, 0.0)
        ei = ei + jnp.where(brow == 0, ci, 0.0)
        for s in range(nsteps - inner):
            k = 1 << s
            dr, di = cmul(a_ref[0, q, 0, inner + s:inner + s + 1, :], a_ref[0, q, 1, inner + s:inner + s + 1, :],
                          _shift_rows(er, k, brow), _shift_rows(ei, k, brow))
            er, ei = er + dr, ei + di
        st_ref[2 * q:2 * q + 1, :] = er[nb - 1:nb, :]
        st_ref[2 * q + 1:2 * q + 2, :] = ei[nb - 1:nb, :]
        pr = jnp.concatenate([jnp.broadcast_to(x0r, (SUBLANES, LANES))]
                             + [jnp.broadcast_to(er[k:k + 1, :], (SUBLANES, LANES)) for k in range(nb - 1)], axis=0)
        pi = jnp.concatenate([jnp.broadcast_to(x0i, (SUBLANES, LANES))]
                             + [jnp.broadcast_to(ei[k:k + 1, :], (SUBLANES, LANES)) for k in range(nb - 1)], axis=0)
        lr = jnp.concatenate([al_ref[0, q, 0]] * nb, axis=0)
        li = jnp.concatenate([al_ref[0, q, 1]] * nb, axis=0)
        dr, di = cmul(lr, li, pr, pi)
        xr, xi = xr + dr, xi + di
        y = y + _dot(xr.astype(BF16), c_ref[0, 2 * q]) + _dot(xi.astype(BF16), c_ref[0, 2 * q + 1])
    o_ref[0] = jax.nn.gelu(y).astype(o_ref.dtype)


def _s5_params(lam_re, lam_im, log_dt, b_re, b_im, c_re, c_im, nsteps):
    lr = lam_re.astype(F32)
    li = lam_im.astype(F32)
    dt = jnp.exp(log_dt.astype(F32))[:, None]
    mag = jnp.exp(lr * dt)
    ang = li * dt
    ab_re = mag * jnp.cos(ang)
    ab_im = mag * jnp.sin(ang)
    den = lr * lr + li * li
    cr = ((ab_re - 1.0) * lr + ab_im * li) / den
    ci = (ab_im * lr - (ab_re - 1.0) * li) / den
    br = b_re.astype(F32)
    bi = b_im.astype(F32)
    bb_re = cr[..., None] * br - ci[..., None] * bi
    bb_im = cr[..., None] * bi + ci[..., None] * br
    nblk = D_MODEL // LANES
    gpb = LANES // S5_GROUP
    npair = gpb // 2
    onehot = (jnp.arange(gpb)[None, None, :] == (2 * jnp.arange(npair)[:, None, None] + jnp.arange(2)[None, :, None]))
    onehot = onehot.astype(F32)

    def bsel(bb):
        bb = bb.reshape(nblk, npair, 2, S5_STATE, S5_GROUP)
        return jnp.einsum('kqmph,qmg->kqghmp', bb, onehot).reshape(nblk, npair, LANES, LANES)

    def csel(cc):
        cc = cc.astype(F32).reshape(nblk, npair, 2, S5_GROUP, S5_STATE)
        return jnp.einsum('kqmhp,qmg->kqmpgh', cc, onehot).reshape(nblk, npair, LANES, LANES)

    b_sel = jnp.stack([bsel(bb_re), bsel(bb_im)], axis=2).reshape(nblk, 2 * npair, LANES, LANES).astype(BF16)
    c_sel = jnp.stack([csel(c_re), -csel(c_im)], axis=2).reshape(nblk, 2 * npair, LANES, LANES).astype(BF16)
    ar = ab_re.reshape(-1, LANES)
    ai = ab_im.reshape(-1, LANES)
    pr, pi = [], []
    for _ in range(nsteps):
        pr.append(ar)
        pi.append(ai)
        ar, ai = ar * ar - ai * ai, 2.0 * ar * ai
    apow = jnp.stack([jnp.stack(pr, axis=1), jnp.stack(pi, axis=1)], axis=1)
    ar = ab_re.reshape(-1, LANES)
    ai = ab_im.reshape(-1, LANES)
    lr, li = [ar], [ai]
    for _ in range(SUBLANES - 1):
        lr, li = lr + [lr[-1] * ar - li[-1] * ai], li + [lr[-1] * ai + li[-1] * ar]
    alin = jnp.stack([jnp.stack(lr, axis=1), jnp.stack(li, axis=1)], axis=1)
    return (b_sel, c_sel, apow.reshape(nblk, npair, 2, nsteps, LANES),
            alin.reshape(nblk, npair, 2, SUBLANES, LANES))


def _s5_core(u, b_sel, c_sel, apow, alin, d_skip, ts):
    b, lp, d = u.shape
    nsteps = apow.shape[3]
    nblk = d // LANES
    return pl.pallas_call(
        functools.partial(_s5_kernel, nsteps=nsteps),
        grid=(b, nblk, lp // ts),
        in_specs=[pl.BlockSpec((1, ts, LANES), lambda bi, k, t: (bi, t, k)),
                  pl.BlockSpec((1, 8, LANES, LANES), lambda bi, k, t: (k, 0, 0, 0)),
                  pl.BlockSpec((1, 8, LANES, LANES), lambda bi, k, t: (k, 0, 0, 0)),
                  pl.BlockSpec((1, 4, 2, nsteps, LANES), lambda bi, k, t: (k, 0, 0, 0, 0)),
                  pl.BlockSpec((1, 4, 2, SUBLANES, LANES), lambda bi, k, t: (k, 0, 0, 0, 0)),
                  pl.BlockSpec((1, LANES), lambda bi, k, t: (0, k))],
        out_specs=pl.BlockSpec((1, ts, LANES), lambda bi, k, t: (bi, t, k)),
        out_shape=jax.ShapeDtypeStruct((b, lp, d), BF16),
        scratch_shapes=[pltpu.VMEM((8, LANES), F32), pltpu.VMEM((2, ts, LANES), F32)],
        compiler_params=_cparams("parallel", "parallel", "arbitrary"),
        name="s5_core",
    )(u, b_sel, c_sel, apow, alin, d_skip.reshape(1, d))


def _s5_layer(h, gain, w_in, lam_re, lam_im, log_dt, b_re, b_im, c_re, c_im, d_skip, w_glu):
    b, lp, d = h.shape
    ts = SEQ_TILE
    b_sel, c_sel, apow, alin = _s5_params(lam_re, lam_im, log_dt, b_re, b_im, c_re, c_im, int(math.log2(ts)))
    h2 = h.reshape(b * lp, d)
    u = _norm_proj(h2, gain, w_in.astype(BF16), F32).reshape(b, lp, d)
    y = _s5_core(u, b_sel, c_sel, apow, alin, d_skip.astype(F32), ts)
    return _glu_out_proj(h2, y.reshape(b * lp, d), w_glu.astype(BF16)).reshape(b, lp, d)


def _rg_kernel(h_ref, hp_ref, gain_ref, win_ref, cw_ref, cb_ref, wa_ref, ba_ref, wi_ref, bi_ref, lam_ref, wout_ref,
               o_ref, xc_ref, xcb_ref, st_ref, blk_ref, y_ref, *, nsteps, halo):
    i = pl.program_id(1)

    @pl.when(i == 0)
    def _():
        st_ref[...] = jnp.zeros_like(st_ref)

    tr, width = xc_ref.shape
    x = h_ref[0]
    xh = jnp.concatenate([hp_ref[0], x], axis=0)
    hrow = lax.broadcasted_iota(jnp.int32, (halo + tr, 1), 0)
    valid = jnp.logical_or(hrow >= halo, i > 0)
    xnh = jnp.where(valid, _rms(xh, gain_ref[...]), 0.0).astype(BF16)
    xn = xnh[halo:]
    for c, w in _chunks(width, 2 * LANES):
        xx = _dot(xnh, win_ref[:, width + c:width + c + w])
        conv = cb_ref[:, c:c + w] + cw_ref[3:4, c:c + w] * xx[halo:]
        for tap in range(1, 4):
            conv = conv + cw_ref[3 - tap:4 - tap, c:c + w] * pltpu.roll(xx, tap, 0)[halo:]
        xc_ref[:, c:c + w] = conv
        xcb_ref[:, c:c + w] = conv.astype(BF16)

    nb = tr // SUBLANES
    inner = int(math.log2(SUBLANES))
    row_in_block = lax.broadcasted_iota(jnp.int32, (tr, 2 * LANES), 0) & (SUBLANES - 1)
    keeps = [row_in_block >= (1 << s) for s in range(inner)]
    brow = lax.broadcasted_iota(jnp.int32, (nb, 2 * LANES), 0)
    for c, w in _chunks(width, 2 * LANES):
        xcb = xcb_ref[...]
        r = _sigmoid(_dot(xcb, wa_ref[:, c:c + w]) + ba_ref[:, c:c + w])
        ig = _sigmoid(_dot(xcb, wi_ref[:, c:c + w]) + bi_ref[:, c:c + w])
        lam = lam_ref[:, c:c + w]
        softplus_neg_lam = jnp.maximum(-lam, 0.0) + jnp.log1p(jnp.exp(-jnp.abs(lam)))
        log_a = (-RG_C * r) * softplus_neg_lam
        a = jnp.exp(log_a)
        bv = jnp.exp(0.5 * jnp.log(-jnp.tanh(log_a) * (a * a + 1.0))) * (ig * xc_ref[:, c:c + w])
        for s in range(inner):
            k = 1 << s
            keep = keeps[s][:, :w]
            bv = bv + a * jnp.where(keep, pltpu.roll(bv, k, 0), 0.0)
            a = a * jnp.where(keep, pltpu.roll(a, k, 0), 1.0)
        ends = []
        for m, val in enumerate((a, bv)):
            for hf in range(w // LANES):
                blk_ref[2 * m + hf] = val[:, hf * LANES:(hf + 1) * LANES]
            ends.append(jnp.concatenate([blk_ref[2 * m + hf, pl.ds(SUBLANES - 1, nb, stride=SUBLANES), :]
                                         for hf in range(w // LANES)], axis=1))
        ea, eb = ends
        h0 = st_ref[:, c:c + w]
        bw = brow[:, :w]
        eb = eb + jnp.where(bw == 0, ea * h0, 0.0)
        for s in range(nsteps - inner):
            k = 1 << s
            eb = eb + ea * _shift_rows(eb, k, bw)
            ea = ea * _shift_rows(ea, k, bw, fill=1.0)
        st_ref[:, c:c + w] = eb[nb - 1:nb, :]
        enter = jnp.concatenate([jnp.broadcast_to(h0, (SUBLANES, w))]
                                + [jnp.broadcast_to(eb[k:k + 1, :], (SUBLANES, w)) for k in range(nb - 1)], axis=0)
        hh = bv + a * enter
        gate = _dot(xn, win_ref[:, c:c + w])
        y_ref[:, c:c + w] = (hh * jax.nn.gelu(gate)).astype(BF16)
    y = y_ref[...]
    for c, w in _chunks(x.shape[1], 512):
        o_ref[0, :, c:c + w] = x[:, c:c + w] + _dot(y, wout_ref[:, c:c + w])


def _rg_call(h, gain, w_in, conv_w, conv_b, w_a, b_a, w_i, b_i, lam, w_out, tr):
    b, lp, d = h.shape
    width = w_out.shape[0]
    halo = SUBLANES
    hb = tr // halo
    vec = lambda v: v.reshape(1, width)
    return pl.pallas_call(
        functools.partial(_rg_kernel, nsteps=int(math.log2(tr)), halo=halo),
        grid=(b, lp // tr),
        in_specs=[pl.BlockSpec((1, tr, d), lambda bi, i: (bi, i, 0)),
                  pl.BlockSpec((1, halo, d), lambda bi, i: (bi, jnp.maximum(i * hb - 1, 0), 0)),
                  _const_spec((1, d)), _const_spec((d, 2 * width)),
                  _const_spec((4, width)), _const_spec((1, width)),
                  _const_spec((width, width)), _const_spec((1, width)),
                  _const_spec((width, width)), _const_spec((1, width)), _const_spec((1, width)),
                  _const_spec((width, d))],
        out_specs=pl.BlockSpec((1, tr, d), lambda bi, i: (bi, i, 0)),
        out_shape=jax.ShapeDtypeStruct((b, lp, d), F32),
        scratch_shapes=[pltpu.VMEM((tr, width), F32), pltpu.VMEM((tr, width), BF16), pltpu.VMEM((1, width), F32),
                        pltpu.VMEM((4, tr, LANES), F32), pltpu.VMEM((tr, width), BF16)],
        compiler_params=_cparams("parallel", "arbitrary"),
        name="rg_layer",
    )(h, h, gain.reshape(1, d), w_in, conv_w, vec(conv_b), w_a, vec(b_a), w_i, vec(b_i), vec(lam), w_out)


def _block_diag(w):
    n, j, k = w.shape
    return jnp.einsum('njk,nm->njmk', w, jnp.eye(n, dtype=w.dtype)).reshape(n * j, n * k)


def _rg_layer(h, gain, w_in, conv_w, conv_b, w_a, b_a, w_i, b_i, lam, w_out):
    b, lp, d = h.shape
    width = w_out.shape[0]
    wp = -(-width // LANES) * LANES
    pad = wp - width
    padc = lambda v: jnp.pad(v.astype(F32), [(0, 0)] * (v.ndim - 1) + [(0, pad)])
    w_in_p = jnp.concatenate([padc(w_in[:, :width]), padc(w_in[:, width:])], axis=1).astype(BF16)
    dense = lambda w: jnp.pad(_block_diag(w.astype(F32)), ((0, pad), (0, pad))).astype(BF16)
    w_out_p = jnp.pad(w_out.astype(F32), ((0, pad), (0, 0))).astype(BF16)
    return _rg_call(h, gain, w_in_p, padc(conv_w), padc(conv_b), dense(w_a), padc(b_a), dense(w_i), padc(b_i),
                    padc(lam), w_out_p, SEQ_TILE)


def _sb_kernel(q_ref, k_ref, v_ref, tri_ref, bias_ref, o_ref, acc_ref, later_ref, *, last_rows):
    i = pl.program_id(2)
    nq = pl.num_programs(2)
    tq = q_ref.shape[1]
    lane = lax.broadcasted_iota(jnp.int32, (1, LANES), 1)
    head0 = lane < SB_HEAD_DIM
    tri = tri_ref[...]

    def tile_start(j):
        return pl.multiple_of(jnp.maximum(j, 0) * tq, tq)

    def weights(z, later):
        neg_abs = lax.bitcast_convert_type(lax.bitcast_convert_type(z, jnp.uint32) | jnp.uint32(0x80000000), F32)
        log_beta = jnp.minimum(z, 0.0) - jnp.log(1.0 + jnp.exp(neg_abs))
        log_keep = log_beta - z
        between = _dot(log_keep.astype(BF16), tri)
        w = jnp.exp(log_beta + between + later)
        return w.astype(BF16), jnp.sum(log_keep, axis=-1, keepdims=True)

    def run(rows):
        q = q_ref[0, :rows, :]
        zero = jnp.zeros_like(q)
        q2 = jnp.concatenate([jnp.where(head0, q, zero), jnp.where(head0, zero, q)], axis=0)
        acc_ref[:2 * rows] = jnp.zeros((2 * rows, LANES), F32)
        later_ref[:2 * rows] = jnp.zeros((2 * rows, LANES), F32)

        def bias(diagonal):
            idx = jnp.where(diagonal, 0, 1)
            if rows == tq:
                return bias_ref[idx]
            return jnp.concatenate([bias_ref[idx, :rows], bias_ref[idx, tq:tq + rows]], axis=0)

        def group(j, n, first_is_diagonal):
            later = later_ref[:2 * rows, :1]
            pv = None
            for t in range(n):
                z = _dot_nt(q2, k_ref[0, pl.ds(tile_start(j - t), tq), :])
                if t == 0:
                    z = z + bias(first_is_diagonal)
                w, rs = weights(z, later)
                part = _dot(w, v_ref[0, pl.ds(tile_start(j - t), tq), :])
                pv = part if pv is None else pv + part
                later = later + rs
            acc_ref[:2 * rows] += pv
            later_ref[:2 * rows] = jnp.broadcast_to(later, (2 * rows, LANES))

        n_tiles = i + 1
        n_full = n_tiles // SB_GROUP

        def body(g, carry):
            group(i - SB_GROUP * g, SB_GROUP, g == 0)
            return carry

        lax.fori_loop(0, n_full, body, 0)
        rem = n_tiles - SB_GROUP * n_full
        size = SB_GROUP // 2
        while size >= 1:
            @pl.when((rem & size) != 0)
            def _(size=size):
                first = (rem & (2 * size - 1)) - 1
                group(first, size, first == i)
            size //= 2

        out = jnp.where(head0, acc_ref[:rows], acc_ref[rows:2 * rows]).astype(o_ref.dtype)
        if rows == tq:
            o_ref[0] = out
        else:
            o_ref[0] = jnp.zeros(o_ref.shape[1:], o_ref.dtype)
            o_ref[0, :rows, :] = out

    if last_rows == tq:
        run(tq)
    else:
        @pl.when(i < nq - 1)
        def _():
            run(tq)

        @pl.when(i == nq - 1)
        def _():
            if last_rows > 0:
                run(last_rows)
            else:
                o_ref[0] = jnp.zeros(o_ref.shape[1:], o_ref.dtype)


def _sb_core(qkv, tq, length):
    b, lp, d3 = qkv.shape
    d = d3 // 3
    npair = d // LANES
    idx = jnp.arange(tq)
    tri = (idx[:, None] > idx[None, :]).astype(BF16)
    diag = jnp.where(idx[None, :] < idx[:, None], 0.0, SB_MASK_BIAS).astype(F32)
    bias = jnp.stack([jnp.concatenate([diag, diag], axis=0), jnp.zeros((2 * tq, tq), F32)])
    last_rows = max(0, min(tq, -(-(length - (lp // tq - 1) * tq) // BF16_ROWS) * BF16_ROWS))
    return pl.pallas_call(
        functools.partial(_sb_kernel, last_rows=last_rows),
        grid=(b, npair, lp // tq),
        in_specs=[pl.BlockSpec((1, tq, LANES), lambda bi, p, i: (bi, i, p)),
                  pl.BlockSpec((1, lp, LANES), lambda bi, p, i: (bi, 0, npair + p)),
                  pl.BlockSpec((1, lp, LANES), lambda bi, p, i: (bi, 0, 2 * npair + p)),
                  _const_spec((tq, tq)), _const_spec((2, 2 * tq, tq))],
        out_specs=pl.BlockSpec((1, tq, LANES), lambda bi, p, i: (bi, i, p)),
        out_shape=jax.ShapeDtypeStruct((b, lp, d), BF16),
        scratch_shapes=[pltpu.VMEM((2 * tq, LANES), F32), pltpu.VMEM((2 * tq, LANES), F32)],
        compiler_params=_cparams("parallel", "parallel", "arbitrary"),
        name="sb_core",
    )(qkv, qkv, qkv, tri, bias)


def _sb_layer(h, gain, w_qkv, q_norm, k_norm, w_out, length):
    b, lp, d = h.shape
    h2 = h.reshape(b * lp, d)
    qkv = _qkv_proj(h2, gain, w_qkv.astype(BF16), q_norm, k_norm).reshape(b, lp, 3 * d)
    o = _sb_core(qkv, SEQ_TILE, length)
    return _out_proj(h2, o.reshape(b * lp, d), w_out.astype(BF16)).reshape(b, lp, d)


def _hg_kernel(h_ref, gain_ref, win_ref, gam_ref, on_ref, wout_ref, o_ref, st_ref, y_ref, *, layer):
    t = pl.program_id(1)

    @pl.when(t == 0)
    def _():
        st_ref[...] = jnp.zeros_like(st_ref)

    gam = gam_ref[...]
    e = jnp.exp(gam - jnp.max(gam, axis=0, keepdims=True))
    lb = jnp.sum(e[:layer], axis=0, keepdims=True) / jnp.sum(e, axis=0, keepdims=True)
    x = h_ref[0]
    th, d = x.shape
    xn = _rms(x, gain_ref[...]).astype(BF16)

    def proj(n):
        return jnp.concatenate([_dot(xn, win_ref[:, n * d + c:n * d + c + w]) for c, w in _chunks(d, 512)], axis=1)

    fz = proj(1)
    sig = _sigmoid(fz)
    logf = jnp.log(lb + (1.0 - lb) * sig)
    kk = (1.0 - lb) * (1.0 - sig)
    row = lax.broadcasted_iota(jnp.int32, (th, d), 0)
    cum = logf
    for s in range(int(math.log2(th))):
        cum = cum + _shift_rows(cum, 1 << s, row)
    total = cum[th - 1:th, :]
    qv = proj(0)
    vv = proj(2)
    gate = proj(3)
    rowc = lax.broadcasted_iota(jnp.int32, (th, th), 0)
    colc = lax.broadcasted_iota(jnp.int32, (th, th), 1)

    levels = []
    w = th // 2
    while w >= HG_BLOCK:
        blk = 2 * w
        ref = jnp.concatenate([jnp.broadcast_to(cum[s0 + w - 1:s0 + w, :], (blk, d)) for s0 in range(0, th, blk)],
                              axis=0)
        qw = (qv * jnp.exp(jnp.minimum(cum - ref, 0.0))).astype(BF16)
        kw = (kk * jnp.exp(jnp.minimum(ref - cum, 0.0))).astype(BF16)
        sh = int(math.log2(blk))
        mask = jnp.logical_and((rowc >> sh) == (colc >> sh),
                               jnp.logical_and((rowc & (blk - 1)) >= w, (colc & (blk - 1)) < w))
        levels.append((qw, kw, mask))
        w //= 2
    ref = jnp.concatenate([jnp.zeros((HG_BLOCK, d), F32)]
                          + [jnp.broadcast_to(cum[s0 - 1:s0, :], (HG_BLOCK, d)) for s0 in range(HG_BLOCK, th, HG_BLOCK)],
                          axis=0)
    qd = (qv * jnp.exp(cum - ref)).astype(BF16)
    kd = (kk * jnp.exp(ref - cum)).astype(BF16)
    sh = int(math.log2(HG_BLOCK))
    levels.append((qd, kd, jnp.logical_and((rowc >> sh) == (colc >> sh), colc <= rowc)))
    q_in = (qv * jnp.exp(cum)).astype(BF16)
    k_out = (kk * jnp.exp(total - cum)).astype(BF16)
    dec = jnp.exp(total)

    for hd in range(d // HG_HEAD_DIM):
        sl = slice(hd * HG_HEAD_DIM, (hd + 1) * HG_HEAD_DIM)
        scores = jnp.zeros((th, th), F32)
        for qw, kw, mask in levels:
            scores = scores + jnp.where(mask, _dot_nt(qw[:, sl], kw[:, sl]), 0.0)
        vh = vv[:, sl].astype(BF16)
        st = st_ref[hd]
        o = _dot(scores.astype(BF16), vh) + _dot_nt(q_in[:, sl], st.astype(BF16))
        st_ref[hd] = st * dec[:, sl] + _dot_tn(vh, k_out[:, sl])
        on = o * lax.rsqrt(jnp.mean(o * o, axis=-1, keepdims=True) + NORM_EPS) * on_ref[:, sl]
        y_ref[:, sl] = (on * _silu(gate[:, sl])).astype(BF16)
    y = y_ref[...]
    for c, w in _chunks(d, 512):
        o_ref[0, :, c:c + w] = x[:, c:c + w] + _dot(y, wout_ref[:, c:c + w])


def _hg_layer(h, gain, w_in, gamma, layer, o_norm, w_out):
    b, lp, d = h.shape
    depth = gamma.shape[0]
    th = SEQ_TILE
    return pl.pallas_call(
        functools.partial(_hg_kernel, layer=layer),
        grid=(b, lp // th),
        in_specs=[pl.BlockSpec((1, th, d), lambda bi, t: (bi, t, 0)), _const_spec((1, d)), _const_spec((d, 4 * d)),
                  _const_spec((depth, d)), _const_spec((1, d)), _const_spec((d, d))],
        out_specs=pl.BlockSpec((1, th, d), lambda bi, t: (bi, t, 0)),
        out_shape=jax.ShapeDtypeStruct((b, lp, d), F32),
        scratch_shapes=[pltpu.VMEM((d // HG_HEAD_DIM, HG_HEAD_DIM, HG_HEAD_DIM), F32), pltpu.VMEM((th, d), BF16)],
        compiler_params=_cparams("parallel", "arbitrary"),
        name="hg_layer",
    )(h, gain.reshape(1, d), w_in.astype(BF16), gamma.astype(F32), o_norm.astype(F32).reshape(1, d),
      w_out.astype(BF16))


def kernel(x, meta_tokens, norm_mix, norm_ffn, s5_w_in, s5_lam_re, s5_lam_im, s5_log_dt, s5_b_re, s5_b_im, s5_c_re, s5_c_im, s5_d, s5_w_glu, rg_w_in, rg_conv_w, rg_conv_b, rg_w_a, rg_b_a, rg_w_i, rg_b_i, rg_lambda, rg_w_out, sb_w_qkv, sb_q_norm, sb_k_norm, sb_w_out, hg_w_in, hg_gamma, hg_o_norm, hg_w_out, ffn_w_up, ffn_conv_w, ffn_conv_b, ffn_w_down):
    bsz, seq, d = x.shape
    n_meta = meta_tokens.shape[0]
    length = n_meta + seq
    assert ffn_conv_w.shape[1] == FFN_CONV
    lp = -(-(length + FFN_CONV - 1) // SEQ_TILE) * SEQ_TILE
    meta =jnp.broadcast_to(meta_tokens.astype(x.dtype)[None], (bsz, n_meta, d))
    h = jnp.concatenate([meta, x, jnp.zeros((bsz, lp - length, d), x.dtype)], axis=1)
    depth = norm_mix.shape[0]
    n_mixers = 4
    for layer in range(depth):
        m, j = layer % n_mixers, layer // n_mixers
        g = norm_mix[layer].astype(F32)
        if m == 0:
            h = _s5_layer(h, g, s5_w_in[j], s5_lam_re[j], s5_lam_im[j], s5_log_dt[j], s5_b_re[j], s5_b_im[j],
                          s5_c_re[j], s5_c_im[j], s5_d[j], s5_w_glu[j])
        elif m == 1:
            h = _rg_layer(h, g, rg_w_in[j], rg_conv_w[j], rg_conv_b[j], rg_w_a[j], rg_b_a[j], rg_w_i[j], rg_b_i[j],
                          rg_lambda[j], rg_w_out[j])
        elif m == 2:
            h = _sb_layer(h, g, sb_w_qkv[j], sb_q_norm[j], sb_k_norm[j], sb_w_out[j], length)
        else:
            h = _hg_layer(h, g, hg_w_in[j], hg_gamma, layer, hg_o_norm[j], hg_w_out[j])
        h = _conv_ffn(h, norm_ffn[layer].astype(F32), ffn_w_up[layer].astype(BF16), ffn_conv_w[layer].astype(F32),
                      ffn_conv_b[layer].astype(F32), ffn_w_down[layer].astype(BF16))
    return h[:, n_meta:length]
```

```python
import functools
import math

import jax
import jax.numpy as jnp
from jax import lax
from jax.experimental import pallas as pl
from jax.experimental.pallas import tpu as pltpu

F32 = jnp.float32
BF16 = jnp.bfloat16

D_MODEL = 1024
N_META = 16
NORM_EPS = 1e-6
LANES = 128
SUBLANES = 8
BF16_ROWS = 16
SEQ_TILE = 256
ROW_TILE = 512
VMEM_LIMIT_BYTES = 56 * 1024 * 1024

FFN_CONV = 3
S5_GROUP = 16
S5_STATE = 64
RG_BLOCKS = 16
RG_C = 8.0
SB_HEAD_DIM = 64
SB_GROUP = 4
SB_MASK_BIAS = -1e9
HG_HEAD_DIM = 128
HG_HEADS = D_MODEL // HG_HEAD_DIM
HG_BLOCK = 16


def _cparams(*sem):
    return pltpu.CompilerParams(dimension_semantics=sem, vmem_limit_bytes=VMEM_LIMIT_BYTES)


def _const_spec(shape):
    nd = len(shape)
    return pl.BlockSpec(shape, lambda *_: (0,) * nd)


def _rms(x, g):
    return x * lax.rsqrt(jnp.mean(x * x, axis=-1, keepdims=True) + NORM_EPS) * g


def _sigmoid(x):
    return 0.5 * jnp.tanh(0.5 * x) + 0.5


def _silu(x):
    half = 0.5 * x
    return half * jnp.tanh(half) + half


def _dot(a, b):
    return jnp.dot(a, b, preferred_element_type=F32)


def _dot_nt(a, b):
    return lax.dot_general(a, b, (((1,), (1,)), ((), ())), preferred_element_type=F32)


def _dot_tn(a, b):
    return lax.dot_general(a, b, (((0,), (0,)), ((), ())), preferred_element_type=F32)


def _shift_rows(x, k, row, fill=0.0):
    n = x.shape[0]
    if k % SUBLANES == 0:
        return jnp.concatenate([jnp.full((k,) + x.shape[1:], fill, x.dtype), x[: n - k]], axis=0)
    return jnp.where(row >= k, pltpu.roll(x, k, 0), fill)


def _chunks(n, width):
    out, c = [], 0
    while c < n:
        w = min(width, n - c)
        out.append((c, w))
        c += w
    return out


def _qkv_kernel(x_ref, g_ref, w_ref, avg_ref, qg_ref, kg_ref, o_ref):
    xn = _rms(x_ref[...], g_ref[...]).astype(BF16)
    d = x_ref.shape[-1]
    for c, w in _chunks(o_ref.shape[-1], 512):
        r = _dot(xn, w_ref[:, c:c + w])
        if c < 2 * d:
            ms = _dot((r * r).astype(BF16), avg_ref[...])
            gain = qg_ref[...] if c < d else kg_ref[...]
            r = r * lax.rsqrt(ms + NORM_EPS) * gain
        o_ref[:, c:c + w] = r.astype(o_ref.dtype)


def _qkv_proj(h2, gain, w, q_norm, k_norm):
    m, d = h2.shape
    n = w.shape[1]
    tm = ROW_TILE if m % ROW_TILE == 0 else SEQ_TILE
    heads_per_chunk = 512 // SB_HEAD_DIM
    blk = jnp.arange(512) // SB_HEAD_DIM
    avg = jnp.where(blk[:, None] == blk[None, :], 1.0 / SB_HEAD_DIM, 0.0).astype(BF16)
    scale = SB_HEAD_DIM ** -0.5
    qg = jnp.tile(q_norm.astype(F32) * scale, heads_per_chunk).reshape(1, 512)
    kg = jnp.tile(k_norm.astype(F32), heads_per_chunk).reshape(1, 512)
    return pl.pallas_call(
        _qkv_kernel,
        grid=(m // tm,),
        in_specs=[pl.BlockSpec((tm, d), lambda i: (i, 0)), _const_spec((1, d)), _const_spec((d, n)),
                  _const_spec((512, 512)), _const_spec((1, 512)), _const_spec((1, 512))],
        out_specs=pl.BlockSpec((tm, n), lambda i: (i, 0)),
        out_shape=jax.ShapeDtypeStruct((m, n), BF16),
        compiler_params=_cparams("parallel"),
        name="qkv_proj",
    )(h2, gain.reshape(1, d), w, avg, qg, kg)


def _out_kernel(h_ref, y_ref, w_ref, o_ref):
    y = y_ref[...]
    for c, w in _chunks(o_ref.shape[-1], 512):
        o_ref[:, c:c + w] = h_ref[:, c:c + w] + _dot(y, w_ref[:, c:c + w])


def _out_proj(h2, y2, w):
    m, d = h2.shape
    k = y2.shape[1]
    tm = ROW_TILE if m % ROW_TILE == 0 else SEQ_TILE
    return pl.pallas_call(
        _out_kernel,
        grid=(m // tm,),
        in_specs=[pl.BlockSpec((tm, d), lambda i: (i, 0)), pl.BlockSpec((tm, k), lambda i: (i, 0)),
                  _const_spec((k, d))],
        out_specs=pl.BlockSpec((tm, d), lambda i: (i, 0)),
        out_shape=jax.ShapeDtypeStruct((m, d), F32),
        compiler_params=_cparams("parallel"),
        name="out_proj",
    )(h2, y2, w)


def _ffn_kernel(h_ref, hp_ref, g_ref, wu_ref, cw_ref, cb_ref, wd_ref, o_ref, gated_ref, *, d_ff, halo, seq_len):
    i = pl.program_id(0)
    xc = h_ref[...]
    tm, d = xc.shape
    x = jnp.concatenate([hp_ref[...], xc], axis=0)
    row = lax.broadcasted_iota(jnp.int32, (halo + tm, 1), 0)
    pos = lax.rem(i * tm - halo + seq_len, seq_len) + row
    pos = jnp.where(pos >= seq_len, pos - seq_len, pos)
    xn = jnp.where(pos < seq_len - (FFN_CONV - 1), _rms(x, g_ref[...]), 0.0).astype(BF16)

    def conv(u, c, w):
        u1 = pltpu.roll(u, 1, 0)
        u2 = pltpu.roll(u, 2, 0)
        return (cb_ref[:, c:c + w] + cw_ref[2:3, c:c + w] * u[halo:] + cw_ref[1:2, c:c + w] * u1[halo:]
                + cw_ref[0:1, c:c + w] * u2[halo:])

    for c, w in _chunks(d_ff, 512):
        a = conv(_dot(xn, wu_ref[:, c:c + w]), c, w)
        b = conv(_dot(xn, wu_ref[:, d_ff + c:d_ff + c + w]), d_ff + c, w)
        gated_ref[:, c:c + w] = (_silu(a) * b).astype(BF16)
    gated = gated_ref[...]
    for c, w in _chunks(d, 512):
        o_ref[:, c:c + w] = xc[:, c:c + w] + _dot(gated, wd_ref[:, c:c + w])


def _conv_ffn(h, gain, w_up, conv_w, conv_b, w_down, layer):
    b, lp, d = h.shape
    d_ff = w_down.shape[1]
    m = b * lp
    tm = ROW_TILE if m % ROW_TILE == 0 else SEQ_TILE
    halo = SUBLANES
    hb = tm // halo
    out = pl.pallas_call(
        functools.partial(_ffn_kernel, d_ff=d_ff, halo=halo, seq_len=lp),
        grid=(m // tm,),
        in_specs=[pl.BlockSpec((tm, d), lambda i: (i, 0)),
                  pl.BlockSpec((halo, d), lambda i: (jnp.maximum(i * hb - 1, 0), 0)),
                  _const_spec((1, d)), pl.BlockSpec((None, d, 2 * d_ff), lambda i: (layer, 0, 0)),
                  _const_spec((3, 2 * d_ff)), _const_spec((1, 2 * d_ff)),
                  pl.BlockSpec((None, d_ff, d), lambda i: (layer, 0, 0))],
        out_specs=pl.BlockSpec((tm, d), lambda i: (i, 0)),
        out_shape=jax.ShapeDtypeStruct((m, d), F32),
        scratch_shapes=[pltpu.VMEM((tm, d_ff), BF16)],
        compiler_params=_cparams("parallel"),
        name="conv_ffn",
    )(h.reshape(m, d), h.reshape(m, d), gain.reshape(1, d), w_up, conv_w, conv_b.reshape(1, -1), w_down)
    return out.reshape(b, lp, d)


def _s5_kernel(h_ref, gain_ref, win_ref, b_ref, c_ref, a_ref, al_ref, d_ref, wglu_ref, o_ref, st_ref, blk_ref, y_ref,
               *, nsteps):
    t = pl.program_id(1)

    @pl.when(t == 0)
    def _():
        st_ref[...] = jnp.zeros_like(st_ref)

    x = h_ref[0]
    ts, d = x.shape
    xn = _rms(x, gain_ref[...]).astype(BF16)
    u_all = jnp.concatenate([_dot(xn, win_ref[:, c:c + w]) for c, w in _chunks(d, 512)], axis=1)
    nb = ts // SUBLANES
    inner = int(math.log2(SUBLANES))
    row = lax.broadcasted_iota(jnp.int32, (ts, LANES), 0)
    row_in_block = row & (SUBLANES - 1)
    brow = lax.broadcasted_iota(jnp.int32, (nb, LANES), 0)
    keeps = [row_in_block >= (1 << s) for s in range(inner)]

    def cmul(ar, ai, xr, xi):
        return ar * xr - ai * xi, ar * xi + ai * xr

    for kb in range(d // LANES):
        sl = slice(kb * LANES, (kb + 1) * LANES)
        base = kb * 2 * (LANES // S5_GROUP // 2)
        u = u_all[:, sl]
        ub = u.astype(BF16)
        y = u * d_ref[:, sl]
        for q in range(4):
            xr = _dot(ub, b_ref[kb, 2 * q])
            xi = _dot(ub, b_ref[kb, 2 * q + 1])
            for s in range(inner):
                k = 1 << s
                keep = keeps[s]
                dr, di = cmul(a_ref[kb, q, 0, s:s + 1, :], a_ref[kb, q, 1, s:s + 1, :],
                              jnp.where(keep, pltpu.roll(xr, k, 0), 0.0), jnp.where(keep, pltpu.roll(xi, k, 0), 0.0))
                xr, xi = xr + dr, xi + di
            blk_ref[0] = xr
            blk_ref[1] = xi
            er = blk_ref[0, pl.ds(SUBLANES - 1, nb, stride=SUBLANES), :]
            ei = blk_ref[1, pl.ds(SUBLANES - 1, nb, stride=SUBLANES), :]
            x0r = st_ref[base + 2 * q:base + 2 * q + 1, :]
            x0i = st_ref[base + 2 * q + 1:base + 2 * q + 2, :]
            cr, ci = cmul(a_ref[kb, q, 0, inner:inner + 1, :], a_ref[kb, q, 1, inner:inner + 1, :], x0r, x0i)
            er = er + jnp.where(brow == 0, cr, 0.0)
            ei = ei + jnp.where(brow == 0, ci, 0.0)
            for s in range(nsteps - inner):
                k = 1 << s
                dr, di = cmul(a_ref[kb, q, 0, inner + s:inner + s + 1, :], a_ref[kb, q, 1, inner + s:inner + s + 1, :],
                              _shift_rows(er, k, brow), _shift_rows(ei, k, brow))
                er, ei = er + dr, ei + di
            st_ref[base + 2 * q:base + 2 * q + 1, :] = er[nb - 1:nb, :]
            st_ref[base + 2 * q + 1:base + 2 * q + 2, :] = ei[nb - 1:nb, :]
            pr = jnp.concatenate([jnp.broadcast_to(x0r, (SUBLANES, LANES))]
                                 + [jnp.broadcast_to(er[k:k + 1, :], (SUBLANES, LANES)) for k in range(nb - 1)], axis=0)
            pi = jnp.concatenate([jnp.broadcast_to(x0i, (SUBLANES, LANES))]
                                 + [jnp.broadcast_to(ei[k:k + 1, :], (SUBLANES, LANES)) for k in range(nb - 1)], axis=0)
            lr = jnp.concatenate([al_ref[kb, q, 0]] * nb, axis=0)
            li = jnp.concatenate([al_ref[kb, q, 1]] * nb, axis=0)
            dr, di = cmul(lr, li, pr, pi)
            xr, xi = xr + dr, xi + di
            y = y + _dot(xr.astype(BF16), c_ref[kb, 2 * q]) + _dot(xi.astype(BF16), c_ref[kb, 2 * q + 1])
        y_ref[:, sl] = jax.nn.gelu(y).astype(BF16)
    y = y_ref[...]
    for c, w in _chunks(d, 512):
        a = _dot(y, wglu_ref[:, c:c + w])
        g = _dot(y, wglu_ref[:, d + c:d + c + w])
        o_ref[0, :, c:c + w] = x[:, c:c + w] + a * _sigmoid(g)


def _s5_params(lam_re, lam_im, log_dt, b_re, b_im, c_re, c_im, nsteps):
    lr = lam_re.astype(F32)
    li = lam_im.astype(F32)
    dt = jnp.exp(log_dt.astype(F32))[:, None]
    mag = jnp.exp(lr * dt)
    ang = li * dt
    ab_re = mag * jnp.cos(ang)
    ab_im = mag * jnp.sin(ang)
    den = lr * lr + li * li
    cr = ((ab_re - 1.0) * lr + ab_im * li) / den
    ci = (ab_im * lr - (ab_re - 1.0) * li) / den
    br = b_re.astype(F32)
    bi = b_im.astype(F32)
    bb_re = cr[..., None] * br - ci[..., None] * bi
    bb_im = cr[..., None] * bi + ci[..., None] * br
    nblk = D_MODEL // LANES
    gpb = LANES // S5_GROUP
    npair = gpb // 2
    onehot = (jnp.arange(gpb)[None, None, :] == (2 * jnp.arange(npair)[:, None, None] + jnp.arange(2)[None, :, None]))
    onehot = onehot.astype(F32)

    def bsel(bb):
        bb = bb.reshape(nblk, npair, 2, S5_STATE, S5_GROUP)
        return jnp.einsum('kqmph,qmg->kqghmp', bb, onehot).reshape(nblk, npair, LANES, LANES)

    def csel(cc):
        cc = cc.astype(F32).reshape(nblk, npair, 2, S5_GROUP, S5_STATE)
        return jnp.einsum('kqmhp,qmg->kqmpgh', cc, onehot).reshape(nblk, npair, LANES, LANES)

    b_sel = jnp.stack([bsel(bb_re), bsel(bb_im)], axis=2).reshape(nblk, 2 * npair, LANES, LANES).astype(BF16)
    c_sel = jnp.stack([csel(c_re), -csel(c_im)], axis=2).reshape(nblk, 2 * npair, LANES, LANES).astype(BF16)
    ar = ab_re.reshape(-1, LANES)
    ai = ab_im.reshape(-1, LANES)
    pr, pi = [], []
    for _ in range(nsteps):
        pr.append(ar)
        pi.append(ai)
        ar, ai = ar * ar - ai * ai, 2.0 * ar * ai
    apow = jnp.stack([jnp.stack(pr, axis=1), jnp.stack(pi, axis=1)], axis=1)
    ar = ab_re.reshape(-1, LANES)
    ai = ab_im.reshape(-1, LANES)
    lr, li = [ar], [ai]
    for _ in range(SUBLANES - 1):
        lr, li = lr + [lr[-1] * ar - li[-1] * ai], li + [lr[-1] * ai + li[-1] * ar]
    alin = jnp.stack([jnp.stack(lr, axis=1), jnp.stack(li, axis=1)], axis=1)
    return (b_sel, c_sel, apow.reshape(nblk, npair, 2, nsteps, LANES),
            alin.reshape(nblk, npair, 2, SUBLANES, LANES))


def _s5_layer(h, gain, w_in, lam_re, lam_im, log_dt, b_re, b_im, c_re, c_im, d_skip, w_glu):
    b, lp, d = h.shape
    ts = SEQ_TILE
    nsteps = int(math.log2(ts))
    b_sel, c_sel, apow, alin = _s5_params(lam_re, lam_im, log_dt, b_re, b_im, c_re, c_im, nsteps)
    nblk = d // LANES
    return pl.pallas_call(
        functools.partial(_s5_kernel, nsteps=nsteps),
        grid=(b, lp // ts),
        in_specs=[pl.BlockSpec((1, ts, d), lambda bi, t: (bi, t, 0)), _const_spec((1, d)), _const_spec((d, d)),
                  _const_spec(b_sel.shape), _const_spec(c_sel.shape), _const_spec(apow.shape),
                  _const_spec(alin.shape), _const_spec((1, d)), _const_spec((d, 2 * d))],
        out_specs=pl.BlockSpec((1, ts, d), lambda bi, t: (bi, t, 0)),
        out_shape=jax.ShapeDtypeStruct((b, lp, d), F32),
        scratch_shapes=[pltpu.VMEM((nblk * SUBLANES, LANES), F32), pltpu.VMEM((2, ts, LANES), F32),
                        pltpu.VMEM((ts, d), BF16)],
        compiler_params=_cparams("parallel", "arbitrary"),
        name="s5_layer",
    )(h, gain.reshape(1, d), w_in.astype(BF16), b_sel, c_sel, apow, alin, d_skip.astype(F32).reshape(1, d),
      w_glu.astype(BF16))


def _rg_kernel(h_ref, hp_ref, gain_ref, win_ref, cw_ref, cb_ref, wa_ref, ba_ref, wi_ref, bi_ref, lam_ref, wout_ref,
               o_ref, xc_ref, xcb_ref, st_ref, blk_ref, y_ref, *, nsteps, halo):
    i = pl.program_id(1)

    @pl.when(i == 0)
    def _():
        st_ref[...] = jnp.zeros_like(st_ref)

    tr, width = xc_ref.shape
    x = h_ref[0]
    xh = jnp.concatenate([hp_ref[0], x], axis=0)
    hrow = lax.broadcasted_iota(jnp.int32, (halo + tr, 1), 0)
    valid = jnp.logical_or(hrow >= halo, i > 0)
    xnh = jnp.where(valid, _rms(xh, gain_ref[...]), 0.0).astype(BF16)
    xn = xnh[halo:]
    for c, w in _chunks(width, 2 * LANES):
        xx = _dot(xnh, win_ref[:, width + c:width + c + w])
        conv = cb_ref[:, c:c + w] + cw_ref[3:4, c:c + w] * xx[halo:]
        for tap in range(1, 4):
            conv = conv + cw_ref[3 - tap:4 - tap, c:c + w] * pltpu.roll(xx, tap, 0)[halo:]
        xc_ref[:, c:c + w] = conv
        xcb_ref[:, c:c + w] = conv.astype(BF16)

    nb = tr // SUBLANES
    inner = int(math.log2(SUBLANES))
    row_in_block = lax.broadcasted_iota(jnp.int32, (tr, 2 * LANES), 0) & (SUBLANES - 1)
    keeps = [row_in_block >= (1 << s) for s in range(inner)]
    brow = lax.broadcasted_iota(jnp.int32, (nb, 2 * LANES), 0)
    for c, w in _chunks(width, 2 * LANES):
        xcb = xcb_ref[...]
        r = _sigmoid(_dot(xcb, wa_ref[:, c:c + w]) + ba_ref[:, c:c + w])
        ig = _sigmoid(_dot(xcb, wi_ref[:, c:c + w]) + bi_ref[:, c:c + w])
        lam = lam_ref[:, c:c + w]
        softplus_neg_lam = jnp.maximum(-lam, 0.0) + jnp.log1p(jnp.exp(-jnp.abs(lam)))
        log_a = (-RG_C * r) * softplus_neg_lam
        a = jnp.exp(log_a)
        bv = jnp.exp(0.5 * jnp.log(-jnp.tanh(log_a) * (a * a + 1.0))) * (ig * xc_ref[:, c:c + w])
        for s in range(inner):
            k = 1 << s
            keep = keeps[s][:, :w]
            bv = bv + a * jnp.where(keep, pltpu.roll(bv, k, 0), 0.0)
            a = a * jnp.where(keep, pltpu.roll(a, k, 0), 1.0)
        ends = []
        for m, val in enumerate((a, bv)):
            for hf in range(w // LANES):
                blk_ref[2 * m + hf] = val[:, hf * LANES:(hf + 1) * LANES]
            ends.append(jnp.concatenate([blk_ref[2 * m + hf, pl.ds(SUBLANES - 1, nb, stride=SUBLANES), :]
                                         for hf in range(w // LANES)], axis=1))
        ea, eb = ends
        h0 = st_ref[:, c:c + w]
        bw = brow[:, :w]
        eb = eb + jnp.where(bw == 0, ea * h0, 0.0)
        for s in range(nsteps - inner):
            k = 1 << s
            eb = eb + ea * _shift_rows(eb, k, bw)
            ea = ea * _shift_rows(ea, k, bw, fill=1.0)
        st_ref[:, c:c + w] = eb[nb - 1:nb, :]
        enter = jnp.concatenate([jnp.broadcast_to(h0, (SUBLANES, w))]
                                + [jnp.broadcast_to(eb[k:k + 1, :], (SUBLANES, w)) for k in range(nb - 1)], axis=0)
        hh = bv + a * enter
        gate = _dot(xn, win_ref[:, c:c + w])
        y_ref[:, c:c + w] = (hh * jax.nn.gelu(gate)).astype(BF16)
    y = y_ref[...]
    for c, w in _chunks(x.shape[1], 512):
        o_ref[0, :, c:c + w] = x[:, c:c + w] + _dot(y, wout_ref[:, c:c + w])


def _rg_call(h, gain, w_in, conv_w, conv_b, w_a, b_a, w_i, b_i, lam, w_out, tr):
    b, lp, d = h.shape
    width = w_out.shape[0]
    halo = SUBLANES
    hb = tr // halo
    vec = lambda v: v.reshape(1, width)
    return pl.pallas_call(
        functools.partial(_rg_kernel, nsteps=int(math.log2(tr)), halo=halo),
        grid=(b, lp // tr),
        in_specs=[pl.BlockSpec((1, tr, d), lambda bi, i: (bi, i, 0)),
                  pl.BlockSpec((1, halo, d), lambda bi, i: (bi, jnp.maximum(i * hb - 1, 0), 0)),
                  _const_spec((1, d)), _const_spec((d, 2 * width)),
                  _const_spec((4, width)), _const_spec((1, width)),
                  _const_spec((width, width)), _const_spec((1, width)),
                  _const_spec((width, width)), _const_spec((1, width)), _const_spec((1, width)),
                  _const_spec((width, d))],
        out_specs=pl.BlockSpec((1, tr, d), lambda bi, i: (bi, i, 0)),
        out_shape=jax.ShapeDtypeStruct((b, lp, d), F32),
        scratch_shapes=[pltpu.VMEM((tr, width), F32), pltpu.VMEM((tr, width), BF16), pltpu.VMEM((1, width), F32),
                        pltpu.VMEM((4, tr, LANES), F32), pltpu.VMEM((tr, width), BF16)],
        compiler_params=_cparams("parallel", "arbitrary"),
        name="rg_layer",
    )(h, h, gain.reshape(1, d), w_in, conv_w, vec(conv_b), w_a, vec(b_a), w_i, vec(b_i), vec(lam), w_out)


def _block_diag(w):
    n, j, k = w.shape
    return jnp.einsum('njk,nm->njmk', w, jnp.eye(n, dtype=w.dtype)).reshape(n * j, n * k)


def _rg_layer(h, gain, w_in, conv_w, conv_b, w_a, b_a, w_i, b_i, lam, w_out):
    b, lp, d = h.shape
    width = w_out.shape[0]
    wp = -(-width // LANES) * LANES
    pad = wp - width
    padc = lambda v: jnp.pad(v.astype(F32), [(0, 0)] * (v.ndim - 1) + [(0, pad)])
    w_in_p = jnp.concatenate([padc(w_in[:, :width]), padc(w_in[:, width:])], axis=1).astype(BF16)
    dense = lambda w: jnp.pad(_block_diag(w.astype(F32)), ((0, pad), (0, pad))).astype(BF16)
    w_out_p = jnp.pad(w_out.astype(F32), ((0, pad), (0, 0))).astype(BF16)
    return _rg_call(h, gain, w_in_p, padc(conv_w), padc(conv_b), dense(w_a), padc(b_a), dense(w_i), padc(b_i),
                    padc(lam), w_out_p, SEQ_TILE)


def _sb_kernel(q_ref, k_ref, v_ref, tri_ref, bias_ref, o_ref, acc_ref, later_ref, *, last_rows):
    i = pl.program_id(2)
    nq = pl.num_programs(2)
    tq = q_ref.shape[1]
    lane = lax.broadcasted_iota(jnp.int32, (1, LANES), 1)
    head0 = lane < SB_HEAD_DIM
    tri = tri_ref[...]

    def tile_start(j):
        return pl.multiple_of(jnp.maximum(j, 0) * tq, tq)

    def weights(z, later):
        neg_abs = lax.bitcast_convert_type(lax.bitcast_convert_type(z, jnp.uint32) | jnp.uint32(0x80000000), F32)
        log_beta = jnp.minimum(z, 0.0) - jnp.log(1.0 + jnp.exp(neg_abs))
        log_keep = log_beta - z
        between = _dot(log_keep.astype(BF16), tri)
        w = jnp.exp(log_beta + between + later)
        return w.astype(BF16), jnp.sum(log_keep, axis=-1, keepdims=True)

    def run(rows):
        q = q_ref[0, :rows, :]
        zero = jnp.zeros_like(q)
        q2 = jnp.concatenate([jnp.where(head0, q, zero), jnp.where(head0, zero, q)], axis=0)
        acc_ref[:2 * rows] = jnp.zeros((2 * rows, LANES), F32)
        later_ref[:2 * rows] = jnp.zeros((2 * rows, LANES), F32)

        def bias(diagonal):
            idx = jnp.where(diagonal, 0, 1)
            if rows == tq:
                return bias_ref[idx]
            return jnp.concatenate([bias_ref[idx, :rows], bias_ref[idx, tq:tq + rows]], axis=0)

        def group(j, n, first_is_diagonal):
            later = later_ref[:2 * rows, :1]
            pv = None
            for t in range(n):
                z = _dot_nt(q2, k_ref[0, pl.ds(tile_start(j - t), tq), :])
                if t == 0:
                    z = z + bias(first_is_diagonal)
                w, rs = weights(z, later)
                part = _dot(w, v_ref[0, pl.ds(tile_start(j - t), tq), :])
                pv = part if pv is None else pv + part
                later = later + rs
            acc_ref[:2 * rows] += pv
            later_ref[:2 * rows] = jnp.broadcast_to(later, (2 * rows, LANES))

        n_tiles = i + 1
        n_full = n_tiles // SB_GROUP

        def body(g, carry):
            group(i - SB_GROUP * g, SB_GROUP, g == 0)
            return carry

        lax.fori_loop(0, n_full, body, 0)
        rem = n_tiles - SB_GROUP * n_full
        size = SB_GROUP // 2
        while size >= 1:
            @pl.when((rem & size) != 0)
            def _(size=size):
                first = (rem & (2 * size - 1)) - 1
                group(first, size, first == i)
            size //= 2

        out = jnp.where(head0, acc_ref[:rows], acc_ref[rows:2 * rows]).astype(o_ref.dtype)
        if rows == tq:
            o_ref[0] = out
        else:
            o_ref[0] = jnp.zeros(o_ref.shape[1:], o_ref.dtype)
            o_ref[0, :rows, :] = out

    if last_rows == tq:
        run(tq)
    else:
        @pl.when(i < nq - 1)
        def _():
            run(tq)

        @pl.when(i == nq - 1)
        def _():
            if last_rows > 0:
                run(last_rows)
            else:
                o_ref[0] = jnp.zeros(o_ref.shape[1:], o_ref.dtype)


def _sb_core(qkv, tq, length):
    b, lp, d3 = qkv.shape
    d = d3 // 3
    npair = d // LANES
    idx = jnp.arange(tq)
    tri = (idx[:, None] > idx[None, :]).astype(BF16)
    diag = jnp.where(idx[None, :] < idx[:, None], 0.0, SB_MASK_BIAS).astype(F32)
    bias = jnp.stack([jnp.concatenate([diag, diag], axis=0), jnp.zeros((2 * tq, tq), F32)])
    last_rows = max(0, min(tq, -(-(length - (lp // tq - 1) * tq) // BF16_ROWS) * BF16_ROWS))
    return pl.pallas_call(
        functools.partial(_sb_kernel, last_rows=last_rows),
        grid=(b, npair, lp // tq),
        in_specs=[pl.BlockSpec((1, tq, LANES), lambda bi, p, i: (bi, i, p)),
                  pl.BlockSpec((1, lp, LANES), lambda bi, p, i: (bi, 0, npair + p)),
                  pl.BlockSpec((1, lp, LANES), lambda bi, p, i: (bi, 0, 2 * npair + p)),
                  _const_spec((tq, tq)), _const_spec((2, 2 * tq, tq))],
        out_specs=pl.BlockSpec((1, tq, LANES), lambda bi, p, i: (bi, i, p)),
        out_shape=jax.ShapeDtypeStruct((b, lp, d), BF16),
        scratch_shapes=[pltpu.VMEM((2 * tq, LANES), F32), pltpu.VMEM((2 * tq, LANES), F32)],
        compiler_params=_cparams("parallel", "parallel", "arbitrary"),
        name="sb_core",
    )(qkv, qkv, qkv, tri, bias)


def _sb_layer(h, gain, w_qkv, q_norm, k_norm, w_out, length):
    b, lp, d = h.shape
    h2 = h.reshape(b * lp, d)
    qkv = _qkv_proj(h2, gain, w_qkv.astype(BF16), q_norm, k_norm).reshape(b, lp, 3 * d)
    o = _sb_core(qkv, SEQ_TILE, length)
    return _out_proj(h2, o.reshape(b * lp, d), w_out.astype(BF16)).reshape(b, lp, d)


def _hg_kernel(h_ref, gain_ref, win_ref, gam_ref, on_ref, wout_ref, o_ref, st_ref, y_ref, *, layer):
    t = pl.program_id(1)

    @pl.when(t == 0)
    def _():
        st_ref[...] = jnp.zeros_like(st_ref)

    gam = gam_ref[...]
    e = jnp.exp(gam - jnp.max(gam, axis=0, keepdims=True))
    lb = jnp.sum(e[:layer], axis=0, keepdims=True) / jnp.sum(e, axis=0, keepdims=True)
    x = h_ref[0]
    th, d = x.shape
    xn = _rms(x, gain_ref[...]).astype(BF16)

    def proj(n):
        return jnp.concatenate([_dot(xn, win_ref[:, n * d + c:n * d + c + w]) for c, w in _chunks(d, 512)], axis=1)

    fz = proj(1)
    sig = _sigmoid(fz)
    logf = jnp.log(lb + (1.0 - lb) * sig)
    kk = (1.0 - lb) * (1.0 - sig)
    row = lax.broadcasted_iota(jnp.int32, (th, d), 0)
    cum = logf
    for s in range(int(math.log2(th))):
        cum = cum + _shift_rows(cum, 1 << s, row)
    total = cum[th - 1:th, :]
    qv = proj(0)
    vv = proj(2)
    gate = proj(3)
    rowc = lax.broadcasted_iota(jnp.int32, (th, th), 0)
    colc = lax.broadcasted_iota(jnp.int32, (th, th), 1)

    levels = []
    w = th // 2
    while w >= HG_BLOCK:
        blk = 2 * w
        ref = jnp.concatenate([jnp.broadcast_to(cum[s0 + w - 1:s0 + w, :], (blk, d)) for s0 in range(0, th, blk)],
                              axis=0)
        qw = (qv * jnp.exp(jnp.minimum(cum - ref, 0.0))).astype(BF16)
        kw = (kk * jnp.exp(jnp.minimum(ref - cum, 0.0))).astype(BF16)
        sh = int(math.log2(blk))
        mask = jnp.logical_and((rowc >> sh) == (colc >> sh),
                               jnp.logical_and((rowc & (blk - 1)) >= w, (colc & (blk - 1)) < w))
        levels.append((qw, kw, mask))
        w //= 2
    ref = jnp.concatenate([jnp.zeros((HG_BLOCK, d), F32)]
                          + [jnp.broadcast_to(cum[s0 - 1:s0, :], (HG_BLOCK, d)) for s0 in range(HG_BLOCK, th, HG_BLOCK)],
                          axis=0)
    qd = (qv * jnp.exp(cum - ref)).astype(BF16)
    kd = (kk * jnp.exp(ref - cum)).astype(BF16)
    sh = int(math.log2(HG_BLOCK))
    levels.append((qd, kd, jnp.logical_and((rowc >> sh) == (colc >> sh), colc <= rowc)))
    q_in = (qv * jnp.exp(cum)).astype(BF16)
    k_out = (kk * jnp.exp(total - cum)).astype(BF16)
    dec = jnp.exp(total)

    for hd in range(d // HG_HEAD_DIM):
        sl = slice(hd * HG_HEAD_DIM, (hd + 1) * HG_HEAD_DIM)
        scores = jnp.zeros((th, th), F32)
        for qw, kw, mask in levels:
            scores = scores + jnp.where(mask, _dot_nt(qw[:, sl], kw[:, sl]), 0.0)
        vh = vv[:, sl].astype(BF16)
        st = st_ref[hd]
        o = _dot(scores.astype(BF16), vh) + _dot_nt(q_in[:, sl], st.astype(BF16))
        st_ref[hd] = st * dec[:, sl] + _dot_tn(vh, k_out[:, sl])
        on = o * lax.rsqrt(jnp.mean(o * o, axis=-1, keepdims=True) + NORM_EPS) * on_ref[:, sl]
        y_ref[:, sl] = (on * _silu(gate[:, sl])).astype(BF16)
    y = y_ref[...]
    for c, w in _chunks(d, 512):
        o_ref[0, :, c:c + w] = x[:, c:c + w] + _dot(y, wout_ref[:, c:c + w])


def _hg_layer(h, gain, w_in, gamma, layer, o_norm, w_out):
    b, lp, d = h.shape
    depth = gamma.shape[0]
    th = SEQ_TILE
    return pl.pallas_call(
        functools.partial(_hg_kernel, layer=layer),
        grid=(b, lp // th),
        in_specs=[pl.BlockSpec((1, th, d), lambda bi, t: (bi, t, 0)), _const_spec((1, d)), _const_spec((d, 4 * d)),
                  _const_spec((depth, d)), _const_spec((1, d)), _const_spec((d, d))],
        out_specs=pl.BlockSpec((1, th, d), lambda bi, t: (bi, t, 0)),
        out_shape=jax.ShapeDtypeStruct((b, lp, d), F32),
        scratch_shapes=[pltpu.VMEM((d // HG_HEAD_DIM, HG_HEAD_DIM, HG_HEAD_DIM), F32), pltpu.VMEM((th, d), BF16)],
        compiler_params=_cparams("parallel", "arbitrary"),
        name="hg_layer",
    )(h, gain.reshape(1, d), w_in.astype(BF16), gamma.astype(F32), o_norm.astype(F32).reshape(1, d),
      w_out.astype(BF16))


def kernel(x, meta_tokens, norm_mix, norm_ffn, s5_w_in, s5_lam_re, s5_lam_im, s5_log_dt, s5_b_re, s5_b_im, s5_c_re, s5_c_im, s5_d, s5_w_glu, rg_w_in, rg_conv_w, rg_conv_b, rg_w_a, rg_b_a, rg_w_i, rg_b_i, rg_lambda, rg_w_out, sb_w_qkv, sb_q_norm, sb_k_norm, sb_w_out, hg_w_in, hg_gamma, hg_o_norm, hg_w_out, ffn_w_up, ffn_conv_w, ffn_conv_b, ffn_w_down):
    bsz, seq, d = x.shape
    n_meta = meta_tokens.shape[0]
    length = n_meta + seq
    assert ffn_conv_w.shape[1] == FFN_CONV
    lp = -(-(length + FFN_CONV - 1) // SEQ_TILE) * SEQ_TILE
    w_up_all = ffn_w_up.astype(BF16)
    w_down_all = ffn_w_down.astype(BF16)
    meta = jnp.broadcast_to(meta_tokens.astype(x.dtype)[None], (bsz, n_meta, d))
    h = jnp.concatenate([meta, x, jnp.zeros((bsz, lp - length, d), x.dtype)], axis=1)
    depth = norm_mix.shape[0]
    n_mixers = 4
    for layer in range(depth):
        m, j = layer % n_mixers, layer // n_mixers
        g = norm_mix[layer].astype(F32)
        if m == 0:
            h = _s5_layer(h, g, s5_w_in[j], s5_lam_re[j], s5_lam_im[j], s5_log_dt[j], s5_b_re[j], s5_b_im[j],
                          s5_c_re[j], s5_c_im[j], s5_d[j], s5_w_glu[j])
        elif m == 1:
            h = _rg_layer(h, g, rg_w_in[j], rg_conv_w[j], rg_conv_b[j], rg_w_a[j], rg_b_a[j], rg_w_i[j], rg_b_i[j],
                          rg_lambda[j], rg_w_out[j])
        elif m == 2:
            h = _sb_layer(h, g, sb_w_qkv[j], sb_q_norm[j], sb_k_norm[j], sb_w_out[j], length)
        else:
            h = _hg_layer(h, g, hg_w_in[j], hg_gamma, layer, hg_o_norm[j], hg_w_out[j])
        h = _conv_ffn(h, norm_ffn[layer].astype(F32), w_up_all, ffn_conv_w[layer].astype(F32),
                      ffn_conv_b[layer].astype(F32), w_down_all, layer)
    return h[:, n_meta:length]
```

```python
import functools
import math

import jax
import jax.numpy as jnp
from jax import lax
from jax.experimental import pallas as pl
from jax.experimental.pallas import tpu as pltpu

F32 = jnp.float32
BF16 = jnp.bfloat16

D_MODEL = 1024
N_META = 16
NORM_EPS = 1e-6
LANES = 128
SUBLANES = 8
BF16_ROWS = 16
SEQ_TILE = 256
ROW_TILE = 512
VMEM_LIMIT_BYTES = 56 * 1024 * 1024

FFN_CONV = 3
S5_GROUP = 16
S5_STATE = 64
RG_BLOCKS = 16
RG_C = 8.0
SB_HEAD_DIM = 64
SB_GROUP = 4
SB_MASK_BIAS = -1e9
HG_HEAD_DIM = 128
HG_HEADS = D_MODEL // HG_HEAD_DIM
HG_BLOCK = 16


def _cparams(*sem):
    return pltpu.CompilerParams(dimension_semantics=sem, vmem_limit_bytes=VMEM_LIMIT_BYTES)


def _const_spec(shape):
    nd = len(shape)
    return pl.BlockSpec(shape, lambda *_: (0,) * nd)


def _rms(x, g):
    return x * lax.rsqrt(jnp.mean(x * x, axis=-1, keepdims=True) + NORM_EPS) * g


def _sigmoid(x):
    return 0.5 * jnp.tanh(0.5 * x) + 0.5


def _silu(x):
    half = 0.5 * x
    return half * jnp.tanh(half) + half


def _dot(a, b):
    return jnp.dot(a, b, preferred_element_type=F32)


def _dot_nt(a, b):
    return lax.dot_general(a, b, (((1,), (1,)), ((), ())), preferred_element_type=F32)


def _dot_tn(a, b):
    return lax.dot_general(a, b, (((0,), (0,)), ((), ())), preferred_element_type=F32)


def _shift_rows(x, k, row, fill=0.0):
    n = x.shape[0]
    if k % SUBLANES == 0:
        return jnp.concatenate([jnp.full((k,) + x.shape[1:], fill, x.dtype), x[: n - k]], axis=0)
    return jnp.where(row >= k, pltpu.roll(x, k, 0), fill)


def _chunks(n, width):
    out, c = [], 0
    while c < n:
        w = min(width, n - c)
        out.append((c, w))
        c += w
    return out


def _qkv_kernel(x_ref, g_ref, w_ref, avg_ref, qg_ref, kg_ref, o_ref):
    xn = _rms(x_ref[...], g_ref[...]).astype(BF16)
    d = x_ref.shape[-1]
    for c, w in _chunks(o_ref.shape[-1], 512):
        r = _dot(xn, w_ref[:, c:c + w])
        if c < 2 * d:
            ms = _dot((r * r).astype(BF16), avg_ref[...])
            gain = qg_ref[...] if c < d else kg_ref[...]
            r = r * lax.rsqrt(ms + NORM_EPS) * gain
        o_ref[:, c:c + w] = r.astype(o_ref.dtype)


def _qkv_proj(h2, gain, w, q_norm, k_norm):
    m, d = h2.shape
    n = w.shape[1]
    tm = ROW_TILE if m % ROW_TILE == 0 else SEQ_TILE
    heads_per_chunk = 512 // SB_HEAD_DIM
    blk = jnp.arange(512) // SB_HEAD_DIM
    avg = jnp.where(blk[:, None] == blk[None, :], 1.0 / SB_HEAD_DIM, 0.0).astype(BF16)
    scale = SB_HEAD_DIM ** -0.5
    qg = jnp.tile(q_norm.astype(F32) * scale, heads_per_chunk).reshape(1, 512)
    kg = jnp.tile(k_norm.astype(F32), heads_per_chunk).reshape(1, 512)
    return pl.pallas_call(
        _qkv_kernel,
        grid=(m // tm,),
        in_specs=[pl.BlockSpec((tm, d), lambda i: (i, 0)), _const_spec((1, d)), _const_spec((d, n)),
                  _const_spec((512, 512)), _const_spec((1, 512)), _const_spec((1, 512))],
        out_specs=pl.BlockSpec((tm, n), lambda i: (i, 0)),
        out_shape=jax.ShapeDtypeStruct((m, n), BF16),
        compiler_params=_cparams("parallel"),
        name="qkv_proj",
    )(h2, gain.reshape(1, d), w, avg, qg, kg)


def _out_kernel(h_ref, y_ref, w_ref, o_ref):
    y = y_ref[...]
    for c, w in _chunks(o_ref.shape[-1], 512):
        o_ref[:, c:c + w] = h_ref[:, c:c + w] + _dot(y, w_ref[:, c:c + w])


def _out_proj(h2, y2, w):
    m, d = h2.shape
    k = y2.shape[1]
    tm = ROW_TILE if m % ROW_TILE == 0 else SEQ_TILE
    return pl.pallas_call(
        _out_kernel,
        grid=(m // tm,),
        in_specs=[pl.BlockSpec((tm, d), lambda i: (i, 0)), pl.BlockSpec((tm, k), lambda i: (i, 0)),
                  _const_spec((k, d))],
        out_specs=pl.BlockSpec((tm, d), lambda i: (i, 0)),
        out_shape=jax.ShapeDtypeStruct((m, d), F32),
        compiler_params=_cparams("parallel"),
        name="out_proj",
    )(h2, y2, w)


def _ffn_kernel(h_ref, hp_ref, g_ref, wu_ref, cw_ref, cb_ref, wd_ref, o_ref, gated_ref, *, d_ff, halo, seq_len):
    i = pl.program_id(0)
    xc = h_ref[...]
    tm, d = xc.shape
    x = jnp.concatenate([hp_ref[...], xc], axis=0)
    row = lax.broadcasted_iota(jnp.int32, (halo + tm, 1), 0)
    pos = lax.rem(i * tm - halo + seq_len, seq_len) + row
    pos = jnp.where(pos >= seq_len, pos - seq_len, pos)
    xn = jnp.where(pos < seq_len - (FFN_CONV - 1), _rms(x, g_ref[...]), 0.0).astype(BF16)

    def conv(u, c, w):
        u1 = pltpu.roll(u, 1, 0)
        u2 = pltpu.roll(u, 2, 0)
        return (cb_ref[:, c:c + w] + cw_ref[2:3, c:c + w] * u[halo:] + cw_ref[1:2, c:c + w] * u1[halo:]
                + cw_ref[0:1, c:c + w] * u2[halo:])

    for c, w in _chunks(d_ff, 512):
        a = conv(_dot(xn, wu_ref[:, c:c + w]), c, w)
        b = conv(_dot(xn, wu_ref[:, d_ff + c:d_ff + c + w]), d_ff + c, w)
        gated_ref[:, c:c + w] = (_silu(a) * b).astype(BF16)
    gated = gated_ref[...]
    for c, w in _chunks(d, 512):
        o_ref[:, c:c + w] = xc[:, c:c + w] + _dot(gated, wd_ref[:, c:c + w])


def _conv_ffn(h, gain, w_up, conv_w, conv_b, w_down, layer):
    b, lp, d = h.shape
    d_ff = w_down.shape[1]
    m = b * lp
    tm = ROW_TILE if m % ROW_TILE == 0 else SEQ_TILE
    halo = SUBLANES
    hb = tm // halo
    out = pl.pallas_call(
        functools.partial(_ffn_kernel, d_ff=d_ff, halo=halo, seq_len=lp),
        grid=(m // tm,),
        in_specs=[pl.BlockSpec((tm, d), lambda i: (i, 0)),
                  pl.BlockSpec((halo, d), lambda i: (jnp.maximum(i * hb - 1, 0), 0)),
                  _const_spec((1, d)), pl.BlockSpec((None, d, 2 * d_ff), lambda i: (layer, 0, 0)),
                  _const_spec((3, 2 * d_ff)), _const_spec((1, 2 * d_ff)),
                  pl.BlockSpec((None, d_ff, d), lambda i: (layer, 0, 0))],
        out_specs=pl.BlockSpec((tm, d), lambda i: (i, 0)),
        out_shape=jax.ShapeDtypeStruct((m, d), F32),
        scratch_shapes=[pltpu.VMEM((tm, d_ff), BF16)],
        compiler_params=_cparams("parallel"),
        name="conv_ffn",
    )(h.reshape(m, d), h.reshape(m, d), gain.reshape(1, d), w_up, conv_w, conv_b.reshape(1, -1), w_down)
    return out.reshape(b, lp, d)


def _s5_kernel(h_ref, gain_ref, win_ref, b_ref, c_ref, a_ref, al_ref, d_ref, wglu_ref, o_ref, st_ref, blk_ref, y_ref,
               *, nsteps):
    t = pl.program_id(1)

    @pl.when(t == 0)
    def _():
        st_ref[...] = jnp.zeros_like(st_ref)

    x = h_ref[0]
    ts, d = x.shape
    xn = _rms(x, gain_ref[...]).astype(BF16)
    u_all = jnp.concatenate([_dot(xn, win_ref[:, c:c + w]) for c, w in _chunks(d, 512)], axis=1)
    nb = ts // SUBLANES
    inner = int(math.log2(SUBLANES))
    row = lax.broadcasted_iota(jnp.int32, (ts, LANES), 0)
    row_in_block = row & (SUBLANES - 1)
    brow = lax.broadcasted_iota(jnp.int32, (nb, LANES), 0)
    keeps = [row_in_block >= (1 << s) for s in range(inner)]

    def cmul(ar, ai, xr, xi):
        return ar * xr - ai * xi, ar * xi + ai * xr

    for kb in range(d // LANES):
        sl = slice(kb * LANES, (kb + 1) * LANES)
        base = kb * 2 * (LANES // S5_GROUP // 2)
        u = u_all[:, sl]
        ub = u.astype(BF16)
        y = u * d_ref[:, sl]
        for q in range(4):
            xr = _dot(ub, b_ref[kb, 2 * q])
            xi = _dot(ub, b_ref[kb, 2 * q + 1])
            for s in range(inner):
                k = 1 << s
                keep = keeps[s]
                dr, di = cmul(a_ref[kb, q, 0, s:s + 1, :], a_ref[kb, q, 1, s:s + 1, :],
                              jnp.where(keep, pltpu.roll(xr, k, 0), 0.0), jnp.where(keep, pltpu.roll(xi, k, 0), 0.0))
                xr, xi = xr + dr, xi + di
            blk_ref[0] = xr
            blk_ref[1] = xi
            er = blk_ref[0, pl.ds(SUBLANES - 1, nb, stride=SUBLANES), :]
            ei = blk_ref[1, pl.ds(SUBLANES - 1, nb, stride=SUBLANES), :]
            x0r = st_ref[base + 2 * q:base + 2 * q + 1, :]
            x0i = st_ref[base + 2 * q + 1:base + 2 * q + 2, :]
            cr, ci = cmul(a_ref[kb, q, 0, inner:inner + 1, :], a_ref[kb, q, 1, inner:inner + 1, :], x0r, x0i)
            er = er + jnp.where(brow == 0, cr, 0.0)
            ei = ei + jnp.where(brow == 0, ci, 0.0)
            for s in range(nsteps - inner):
                k = 1 << s
                dr, di = cmul(a_ref[kb, q, 0, inner + s:inner + s + 1, :], a_ref[kb, q, 1, inner + s:inner + s + 1, :],
                              _shift_rows(er, k, brow), _shift_rows(ei, k, brow))
                er, ei = er + dr, ei + di
            st_ref[base + 2 * q:base + 2 * q + 1, :] = er[nb - 1:nb, :]
            st_ref[base + 2 * q + 1:base + 2 * q + 2, :] = ei[nb - 1:nb, :]
            pr = jnp.concatenate([jnp.broadcast_to(x0r, (SUBLANES, LANES))]
                                 + [jnp.broadcast_to(er[k:k + 1, :], (SUBLANES, LANES)) for k in range(nb - 1)], axis=0)
            pi = jnp.concatenate([jnp.broadcast_to(x0i, (SUBLANES, LANES))]
                                 + [jnp.broadcast_to(ei[k:k + 1, :], (SUBLANES, LANES)) for k in range(nb - 1)], axis=0)
            lr = jnp.concatenate([al_ref[kb, q, 0]] * nb, axis=0)
            li = jnp.concatenate([al_ref[kb, q, 1]] * nb, axis=0)
            dr, di = cmul(lr, li, pr, pi)
            xr, xi = xr + dr, xi + di
            y = y + _dot(xr.astype(BF16), c_ref[kb, 2 * q]) + _dot(xi.astype(BF16), c_ref[kb, 2 * q + 1])
        y_ref[:, sl] = jax.nn.gelu(y).astype(BF16)
    y = y_ref[...]
    for c, w in _chunks(d, 512):
        a = _dot(y, wglu_ref[:, c:c + w])
        g = _dot(y, wglu_ref[:, d + c:d + c + w])
        o_ref[0, :, c:c + w] = x[:, c:c + w] + a * _sigmoid(g)


def _s5_params(lam_re, lam_im, log_dt, b_re, b_im, c_re, c_im, nsteps):
    lr = lam_re.astype(F32)
    li = lam_im.astype(F32)
    dt = jnp.exp(log_dt.astype(F32))[:, None]
    mag = jnp.exp(lr * dt)
    ang = li * dt
    ab_re = mag * jnp.cos(ang)
    ab_im = mag * jnp.sin(ang)
    den = lr * lr + li * li
    cr = ((ab_re - 1.0) * lr + ab_im * li) / den
    ci = (ab_im * lr - (ab_re - 1.0) * li) / den
    br = b_re.astype(F32)
    bi = b_im.astype(F32)
    bb_re = cr[..., None] * br - ci[..., None] * bi
    bb_im = cr[..., None] * bi + ci[..., None] * br
    nblk = D_MODEL // LANES
    gpb = LANES // S5_GROUP
    npair = gpb // 2
    onehot = (jnp.arange(gpb)[None, None, :] == (2 * jnp.arange(npair)[:, None, None] + jnp.arange(2)[None, :, None]))
    onehot = onehot.astype(F32)

    def bsel(bb):
        bb = bb.reshape(nblk, npair, 2, S5_STATE, S5_GROUP)
        return jnp.einsum('kqmph,qmg->kqghmp', bb, onehot).reshape(nblk, npair, LANES, LANES)

    def csel(cc):
        cc = cc.astype(F32).reshape(nblk, npair, 2, S5_GROUP, S5_STATE)
        return jnp.einsum('kqmhp,qmg->kqmpgh', cc, onehot).reshape(nblk, npair, LANES, LANES)

    b_sel = jnp.stack([bsel(bb_re), bsel(bb_im)], axis=2).reshape(nblk, 2 * npair, LANES, LANES).astype(BF16)
    c_sel = jnp.stack([csel(c_re), -csel(c_im)], axis=2).reshape(nblk, 2 * npair, LANES, LANES).astype(BF16)
    ar = ab_re.reshape(-1, LANES)
    ai = ab_im.reshape(-1, LANES)
    pr, pi = [], []
    for _ in range(nsteps):
        pr.append(ar)
        pi.append(ai)
        ar, ai = ar * ar - ai * ai, 2.0 * ar * ai
    apow = jnp.stack([jnp.stack(pr, axis=1), jnp.stack(pi, axis=1)], axis=1)
    ar = ab_re.reshape(-1, LANES)
    ai = ab_im.reshape(-1, LANES)
    lr, li = [ar], [ai]
    for _ in range(SUBLANES - 1):
        lr, li = lr + [lr[-1] * ar - li[-1] * ai], li + [lr[-1] * ai + li[-1] * ar]
    alin = jnp.stack([jnp.stack(lr, axis=1), jnp.stack(li, axis=1)], axis=1)
    return (b_sel, c_sel, apow.reshape(nblk, npair, 2, nsteps, LANES),
            alin.reshape(nblk, npair, 2, SUBLANES, LANES))


def _s5_layer(h, gain, w_in, lam_re, lam_im, log_dt, b_re, b_im, c_re, c_im, d_skip, w_glu):
    b, lp, d = h.shape
    ts = SEQ_TILE
    nsteps = int(math.log2(ts))
    b_sel, c_sel, apow, alin = _s5_params(lam_re, lam_im, log_dt, b_re, b_im, c_re, c_im, nsteps)
    nblk = d // LANES
    return pl.pallas_call(
        functools.partial(_s5_kernel, nsteps=nsteps),
        grid=(b, lp // ts),
        in_specs=[pl.BlockSpec((1, ts, d), lambda bi, t: (bi, t, 0)), _const_spec((1, d)), _const_spec((d, d)),
                  _const_spec(b_sel.shape), _const_spec(c_sel.shape), _const_spec(apow.shape),
                  _const_spec(alin.shape), _const_spec((1, d)), _const_spec((d, 2 * d))],
        out_specs=pl.BlockSpec((1, ts, d), lambda bi, t: (bi, t, 0)),
        out_shape=jax.ShapeDtypeStruct((b, lp, d), F32),
        scratch_shapes=[pltpu.VMEM((nblk * SUBLANES, LANES), F32), pltpu.VMEM((2, ts, LANES), F32),
                        pltpu.VMEM((ts, d), BF16)],
        compiler_params=_cparams("parallel", "arbitrary"),
        name="s5_layer",
    )(h, gain.reshape(1, d), w_in.astype(BF16), b_sel, c_sel, apow, alin, d_skip.astype(F32).reshape(1, d),
      w_glu.astype(BF16))


def _rg_kernel(h_ref, hp_ref, gain_ref, win_ref, cw_ref, cb_ref, wa_ref, ba_ref, wi_ref, bi_ref, lam_ref, wout_ref,
               o_ref, xc_ref, xcb_ref, st_ref, blk_ref, y_ref, *, nsteps, halo, gate_block):
    i = pl.program_id(1)

    @pl.when(i == 0)
    def _():
        st_ref[...] = jnp.zeros_like(st_ref)

    tr, width = xc_ref.shape
    x = h_ref[0]
    xh = jnp.concatenate([hp_ref[0], x], axis=0)
    hrow = lax.broadcasted_iota(jnp.int32, (halo + tr, 1), 0)
    valid = jnp.logical_or(hrow >= halo, i > 0)
    xnh = jnp.where(valid, _rms(xh, gain_ref[...]), 0.0).astype(BF16)
    xn = xnh[halo:]
    for c, w in _chunks(width, 2 * LANES):
        xx = _dot(xnh, win_ref[:, width + c:width + c + w])
        conv = cb_ref[:, c:c + w] + cw_ref[3:4, c:c + w] * xx[halo:]
        for tap in range(1, 4):
            conv = conv + cw_ref[3 - tap:4 - tap, c:c + w] * pltpu.roll(xx, tap, 0)[halo:]
        xc_ref[:, c:c + w] = conv
        xcb_ref[:, c:c + w] = conv.astype(BF16)

    nb = tr // SUBLANES
    inner = int(math.log2(SUBLANES))
    row_in_block = lax.broadcasted_iota(jnp.int32, (tr, 2 * LANES), 0) & (SUBLANES - 1)
    keeps = [row_in_block >= (1 << s) for s in range(inner)]
    brow = lax.broadcasted_iota(jnp.int32, (nb, 2 * LANES), 0)
    for c, w in _chunks(width, 2 * LANES):
        k0 = (c // gate_block * gate_block) // LANES * LANES
        k1 = min(width, -(-((c + w - 1) // gate_block + 1) * gate_block // LANES) * LANES)
        xcb = xcb_ref[:, k0:k1]
        r = _sigmoid(_dot(xcb, wa_ref[k0:k1, c:c + w]) + ba_ref[:, c:c + w])
        ig = _sigmoid(_dot(xcb, wi_ref[k0:k1, c:c + w]) + bi_ref[:, c:c + w])
        lam = lam_ref[:, c:c + w]
        softplus_neg_lam = jnp.maximum(-lam, 0.0) + jnp.log1p(jnp.exp(-jnp.abs(lam)))
        log_a = (-RG_C * r) * softplus_neg_lam
        a = jnp.exp(log_a)
        bv = jnp.exp(0.5 * jnp.log(-jnp.tanh(log_a) * (a * a + 1.0))) * (ig * xc_ref[:, c:c + w])
        for s in range(inner):
            k = 1 << s
            keep = keeps[s][:, :w]
            bv = bv + a * jnp.where(keep, pltpu.roll(bv, k, 0), 0.0)
            a = a * jnp.where(keep, pltpu.roll(a, k, 0), 1.0)
        ends = []
        for m, val in enumerate((a, bv)):
            for hf in range(w // LANES):
                blk_ref[2 * m + hf] = val[:, hf * LANES:(hf + 1) * LANES]
            ends.append(jnp.concatenate([blk_ref[2 * m + hf, pl.ds(SUBLANES - 1, nb, stride=SUBLANES), :]
                                         for hf in range(w // LANES)], axis=1))
        ea, eb = ends
        h0 = st_ref[:, c:c + w]
        bw = brow[:, :w]
        eb = eb + jnp.where(bw == 0, ea * h0, 0.0)
        for s in range(nsteps - inner):
            k = 1 << s
            eb = eb + ea * _shift_rows(eb, k, bw)
            ea = ea * _shift_rows(ea, k, bw, fill=1.0)
        st_ref[:, c:c + w] = eb[nb - 1:nb, :]
        enter = jnp.concatenate([jnp.broadcast_to(h0, (SUBLANES, w))]
                                + [jnp.broadcast_to(eb[k:k + 1, :], (SUBLANES, w)) for k in range(nb - 1)], axis=0)
        hh = bv + a * enter
        gate = _dot(xn, win_ref[:, c:c + w])
        y_ref[:, c:c + w] = (hh * jax.nn.gelu(gate)).astype(BF16)
    y = y_ref[...]
    for c, w in _chunks(x.shape[1], 512):
        o_ref[0, :, c:c + w] = x[:, c:c + w] + _dot(y, wout_ref[:, c:c + w])


def _rg_call(h, gain, w_in, conv_w, conv_b, w_a, b_a, w_i, b_i, lam, w_out, tr, gate_block):
    b, lp, d = h.shape
    width = w_out.shape[0]
    halo = SUBLANES
    hb = tr // halo
    vec = lambda v: v.reshape(1, width)
    return pl.pallas_call(
        functools.partial(_rg_kernel, nsteps=int(math.log2(tr)), halo=halo, gate_block=gate_block),
        grid=(b, lp // tr),
        in_specs=[pl.BlockSpec((1, tr, d), lambda bi, i: (bi, i, 0)),
                  pl.BlockSpec((1, halo, d), lambda bi, i: (bi, jnp.maximum(i * hb - 1, 0), 0)),
                  _const_spec((1, d)), _const_spec((d, 2 * width)),
                  _const_spec((4, width)), _const_spec((1, width)),
                  _const_spec((width, width)), _const_spec((1, width)),
                  _const_spec((width, width)), _const_spec((1, width)), _const_spec((1, width)),
                  _const_spec((width, d))],
        out_specs=pl.BlockSpec((1, tr, d), lambda bi, i: (bi, i, 0)),
        out_shape=jax.ShapeDtypeStruct((b, lp, d), F32),
        scratch_shapes=[pltpu.VMEM((tr, width), F32), pltpu.VMEM((tr, width), BF16), pltpu.VMEM((1, width), F32),
                        pltpu.VMEM((4, tr, LANES), F32), pltpu.VMEM((tr, width), BF16)],
        compiler_params=_cparams("parallel", "arbitrary"),
        name="rg_layer",
    )(h, h, gain.reshape(1, d), w_in, conv_w, vec(conv_b), w_a, vec(b_a), w_i, vec(b_i), vec(lam), w_out)


def _block_diag(w):
    n, j, k = w.shape
    return jnp.einsum('njk,nm->njmk', w, jnp.eye(n, dtype=w.dtype)).reshape(n * j, n * k)


def _rg_layer(h, gain, w_in, conv_w, conv_b, w_a, b_a, w_i, b_i, lam, w_out):
    b, lp, d = h.shape
    width = w_out.shape[0]
    wp = -(-width // LANES) * LANES
    pad = wp - width
    padc = lambda v: jnp.pad(v.astype(F32), [(0, 0)] * (v.ndim - 1) + [(0, pad)])
    w_in_p = jnp.concatenate([padc(w_in[:, :width]), padc(w_in[:, width:])], axis=1).astype(BF16)
    dense = lambda w: jnp.pad(_block_diag(w.astype(F32)), ((0, pad), (0, pad))).astype(BF16)
    w_out_p = jnp.pad(w_out.astype(F32), ((0, pad), (0, 0))).astype(BF16)
    return _rg_call(h, gain, w_in_p, padc(conv_w), padc(conv_b), dense(w_a), padc(b_a), dense(w_i), padc(b_i),
                    padc(lam), w_out_p, SEQ_TILE, w_a.shape[1])


def _sb_kernel(q_ref, k_ref, v_ref, tri_ref, bias_ref, o_ref, acc_ref, later_ref, *, last_rows):
    i = pl.program_id(2)
    nq = pl.num_programs(2)
    tq = q_ref.shape[1]
    lane = lax.broadcasted_iota(jnp.int32, (1, LANES), 1)
    head0 = lane < SB_HEAD_DIM
    tri = tri_ref[...]

    def tile_start(j):
        return pl.multiple_of(jnp.maximum(j, 0) * tq, tq)

    def weights(z, later):
        neg_abs = lax.bitcast_convert_type(lax.bitcast_convert_type(z, jnp.uint32) | jnp.uint32(0x80000000), F32)
        log_beta = jnp.minimum(z, 0.0) - jnp.log(1.0 + jnp.exp(neg_abs))
        log_keep = log_beta - z
        between = _dot(log_keep.astype(BF16), tri)
        w = jnp.exp(log_beta + between + later)
        return w.astype(BF16), jnp.sum(log_keep, axis=-1, keepdims=True)

    def run(rows):
        q = q_ref[0, :rows, :]
        zero = jnp.zeros_like(q)
        q2 = jnp.concatenate([jnp.where(head0, q, zero), jnp.where(head0, zero, q)], axis=0)
        acc_ref[:2 * rows] = jnp.zeros((2 * rows, LANES), F32)
        later_ref[:2 * rows] = jnp.zeros((2 * rows, LANES), F32)

        def bias(diagonal):
            idx = jnp.where(diagonal, 0, 1)
            if rows == tq:
                return bias_ref[idx]
            return jnp.concatenate([bias_ref[idx, :rows], bias_ref[idx, tq:tq + rows]], axis=0)

        def group(j, n, first_is_diagonal):
            later = later_ref[:2 * rows, :1]
            pv = None
            for t in range(n):
                z = _dot_nt(q2, k_ref[0, pl.ds(tile_start(j - t), tq), :])
                if t == 0:
                    z = z + bias(first_is_diagonal)
                w, rs = weights(z, later)
                part = _dot(w, v_ref[0, pl.ds(tile_start(j - t), tq), :])
                pv = part if pv is None else pv + part
                later = later + rs
            acc_ref[:2 * rows] += pv
            later_ref[:2 * rows] = jnp.broadcast_to(later, (2 * rows, LANES))

        n_tiles = i + 1
        n_full = n_tiles // SB_GROUP

        def body(g, carry):
            group(i - SB_GROUP * g, SB_GROUP, g == 0)
            return carry

        lax.fori_loop(0, n_full, body, 0)
        rem = n_tiles - SB_GROUP * n_full
        size = SB_GROUP // 2
        while size >= 1:
            @pl.when((rem & size) != 0)
            def _(size=size):
                first = (rem & (2 * size - 1)) - 1
                group(first, size, first == i)
            size //= 2

        out = jnp.where(head0, acc_ref[:rows], acc_ref[rows:2 * rows]).astype(o_ref.dtype)
        if rows == tq:
            o_ref[0] = out
        else:
            o_ref[0] = jnp.zeros(o_ref.shape[1:], o_ref.dtype)
            o_ref[0, :rows, :] = out

    if last_rows == tq:
        run(tq)
    else:
        @pl.when(i < nq - 1)
        def _():
            run(tq)

        @pl.when(i == nq - 1)
        def _():
            if last_rows > 0:
                run(last_rows)
            else:
                o_ref[0] = jnp.zeros(o_ref.shape[1:], o_ref.dtype)


def _sb_core(qkv, tq, length):
    b, lp, d3 = qkv.shape
    d = d3 // 3
    npair = d // LANES
    idx = jnp.arange(tq)
    tri = (idx[:, None] > idx[None, :]).astype(BF16)
    diag = jnp.where(idx[None, :] < idx[:, None], 0.0, SB_MASK_BIAS).astype(F32)
    bias = jnp.stack([jnp.concatenate([diag, diag], axis=0), jnp.zeros((2 * tq, tq), F32)])
    last_rows = max(0, min(tq, -(-(length - (lp // tq - 1) * tq) // BF16_ROWS) * BF16_ROWS))
    return pl.pallas_call(
        functools.partial(_sb_kernel, last_rows=last_rows),
        grid=(b, npair, lp // tq),
        in_specs=[pl.BlockSpec((1, tq, LANES), lambda bi, p, i: (bi, i, p)),
                  pl.BlockSpec((1, lp, LANES), lambda bi, p, i: (bi, 0, npair + p)),
                  pl.BlockSpec((1, lp, LANES), lambda bi, p, i: (bi, 0, 2 * npair + p)),
                  _const_spec((tq, tq)), _const_spec((2, 2 * tq, tq))],
        out_specs=pl.BlockSpec((1, tq, LANES), lambda bi, p, i: (bi, i, p)),
        out_shape=jax.ShapeDtypeStruct((b, lp, d), BF16),
        scratch_shapes=[pltpu.VMEM((2 * tq, LANES), F32), pltpu.VMEM((2 * tq, LANES), F32)],
        compiler_params=_cparams("parallel", "parallel", "arbitrary"),
        name="sb_core",
    )(qkv, qkv, qkv, tri, bias)


def _sb_layer(h, gain, w_qkv, q_norm, k_norm, w_out, length):
    b, lp, d = h.shape
    h2 = h.reshape(b * lp, d)
    qkv = _qkv_proj(h2, gain, w_qkv.astype(BF16), q_norm, k_norm).reshape(b, lp, 3 * d)
    o = _sb_core(qkv, SEQ_TILE, length)
    return _out_proj(h2, o.reshape(b * lp, d), w_out.astype(BF16)).reshape(b, lp, d)


def _hg_kernel(h_ref, gain_ref, win_ref, gam_ref, on_ref, wout_ref, o_ref, st_ref, y_ref, *, layer):
    t = pl.program_id(1)

    @pl.when(t == 0)
    def _():
        st_ref[...] = jnp.zeros_like(st_ref)

    gam = gam_ref[...]
    e = jnp.exp(gam - jnp.max(gam, axis=0, keepdims=True))
    lb = jnp.sum(e[:layer], axis=0, keepdims=True) / jnp.sum(e, axis=0, keepdims=True)
    x = h_ref[0]
    th, d = x.shape
    xn = _rms(x, gain_ref[...]).astype(BF16)

    def proj(n):
        return jnp.concatenate([_dot(xn, win_ref[:, n * d + c:n * d + c + w]) for c, w in _chunks(d, 512)], axis=1)

    fz = proj(1)
    sig = _sigmoid(fz)
    logf = jnp.log(lb + (1.0 - lb) * sig)
    kk = (1.0 - lb) * (1.0 - sig)
    row = lax.broadcasted_iota(jnp.int32, (th, d), 0)
    cum = logf
    for s in range(int(math.log2(th))):
        cum = cum + _shift_rows(cum, 1 << s, row)
    total = cum[th - 1:th, :]
    qv = proj(0)
    vv = proj(2)
    gate = proj(3)
    rowc = lax.broadcasted_iota(jnp.int32, (th, th), 0)
    colc = lax.broadcasted_iota(jnp.int32, (th, th), 1)

    levels = []
    w = th // 2
    while w >= HG_BLOCK:
        blk = 2 * w
        ref = jnp.concatenate([jnp.broadcast_to(cum[s0 + w - 1:s0 + w, :], (blk, d)) for s0 in range(0, th, blk)],
                              axis=0)
        qw = (qv * jnp.exp(jnp.minimum(cum - ref, 0.0))).astype(BF16)
        kw = (kk * jnp.exp(jnp.minimum(ref - cum, 0.0))).astype(BF16)
        sh = int(math.log2(blk))
        mask = jnp.logical_and((rowc >> sh) == (colc >> sh),
                               jnp.logical_and((rowc & (blk - 1)) >= w, (colc & (blk - 1)) < w))
        levels.append((qw, kw, mask))
        w //= 2
    ref = jnp.concatenate([jnp.zeros((HG_BLOCK, d), F32)]
                          + [jnp.broadcast_to(cum[s0 - 1:s0, :], (HG_BLOCK, d)) for s0 in range(HG_BLOCK, th, HG_BLOCK)],
                          axis=0)
    qd = (qv * jnp.exp(cum - ref)).astype(BF16)
    kd = (kk * jnp.exp(ref - cum)).astype(BF16)
    sh = int(math.log2(HG_BLOCK))
    levels.append((qd, kd, jnp.logical_and((rowc >> sh) == (colc >> sh), colc <= rowc)))
    q_in = (qv * jnp.exp(cum)).astype(BF16)
    k_out = (kk * jnp.exp(total - cum)).astype(BF16)
    dec = jnp.exp(total)

    for hd in range(d // HG_HEAD_DIM):
        sl = slice(hd * HG_HEAD_DIM, (hd + 1) * HG_HEAD_DIM)
        scores = jnp.zeros((th, th), F32)
        for qw, kw, mask in levels:
            scores = scores + jnp.where(mask, _dot_nt(qw[:, sl], kw[:, sl]), 0.0)
        vh = vv[:, sl].astype(BF16)
        st = st_ref[hd]
        o = _dot(scores.astype(BF16), vh) + _dot_nt(q_in[:, sl], st.astype(BF16))
        st_ref[hd] = st * dec[:, sl] + _dot_tn(vh, k_out[:, sl])
        on = o * lax.rsqrt(jnp.mean(o * o, axis=-1, keepdims=True) + NORM_EPS) * on_ref[:, sl]
        y_ref[:, sl] = (on * _silu(gate[:, sl])).astype(BF16)
    y = y_ref[...]
    for c, w in _chunks(d, 512):
        o_ref[0, :, c:c + w] = x[:, c:c + w] + _dot(y, wout_ref[:, c:c + w])


def _hg_layer(h, gain, w_in, gamma, layer, o_norm, w_out):
    b, lp, d = h.shape
    depth = gamma.shape[0]
    th = SEQ_TILE
    return pl.pallas_call(
        functools.partial(_hg_kernel, layer=layer),
        grid=(b, lp // th),
        in_specs=[pl.BlockSpec((1, th, d), lambda bi, t: (bi, t, 0)), _const_spec((1, d)), _const_spec((d, 4 * d)),
                  _const_spec((depth, d)), _const_spec((1, d)), _const_spec((d, d))],
        out_specs=pl.BlockSpec((1, th, d), lambda bi, t: (bi, t, 0)),
        out_shape=jax.ShapeDtypeStruct((b, lp, d), F32),
        scratch_shapes=[pltpu.VMEM((d // HG_HEAD_DIM, HG_HEAD_DIM, HG_HEAD_DIM), F32), pltpu.VMEM((th, d), BF16)],
        compiler_params=_cparams("parallel", "arbitrary"),
        name="hg_layer",
    )(h, gain.reshape(1, d), w_in.astype(BF16), gamma.astype(F32), o_norm.astype(F32).reshape(1, d),
      w_out.astype(BF16))


def kernel(x, meta_tokens, norm_mix, norm_ffn, s5_w_in, s5_lam_re, s5_lam_im, s5_log_dt, s5_b_re, s5_b_im, s5_c_re, s5_c_im, s5_d, s5_w_glu, rg_w_in, rg_conv_w, rg_conv_b, rg_w_a, rg_b_a, rg_w_i, rg_b_i, rg_lambda, rg_w_out, sb_w_qkv, sb_q_norm, sb_k_norm, sb_w_out, hg_w_in, hg_gamma, hg_o_norm, hg_w_out, ffn_w_up, ffn_conv_w, ffn_conv_b, ffn_w_down):
    bsz, seq, d = x.shape
    n_meta = meta_tokens.shape[0]
    length = n_meta + seq
    assert ffn_conv_w.shape[1] == FFN_CONV
    lp = -(-(length + FFN_CONV - 1) // SEQ_TILE) * SEQ_TILE
    w_up_all = ffn_w_up.astype(BF16)
    w_down_all = ffn_w_down.astype(BF16)
    meta = jnp.broadcast_to(meta_tokens.astype(x.dtype)[None], (bsz, n_meta, d))
    h = jnp.concatenate([meta, x, jnp.zeros((bsz, lp - length, d), x.dtype)], axis=1)
    depth = norm_mix.shape[0]
    n_mixers = 4
    for layer in range(depth):
        m, j = layer % n_mixers, layer // n_mixers
        g = norm_mix[layer].astype(F32)
        if m == 0:
            h = _s5_layer(h, g, s5_w_in[j], s5_lam_re[j], s5_lam_im[j], s5_log_dt[j], s5_b_re[j], s5_b_im[j],
                          s5_c_re[j], s5_c_im[j], s5_d[j], s5_w_glu[j])
        elif m == 1:
            h = _rg_layer(h, g, rg_w_in[j], rg_conv_w[j], rg_conv_b[j], rg_w_a[j], rg_b_a[j], rg_w_i[j], rg_b_i[j],
                          rg_lambda[j], rg_w_out[j])
        elif m == 2:
            h = _sb_layer(h, g, sb_w_qkv[j], sb_q_norm[j], sb_k_norm[j], sb_w_out[j], length)
        else:
            h = _hg_layer(h, g, hg_w_in[j], hg_gamma, layer, hg_o_norm[j], hg_w_out[j])
        h = _conv_ffn(h, norm_ffn[layer].astype(F32), w_up_all, ffn_conv_w[layer].astype(F32),
                      ffn_conv_b[layer].astype(F32), w_down_all, layer)
    return h[:, n_meta:length]
```
